```python
import numpy as np
import jax, jax.numpy as jnp
from jax import lax


D_MODEL = 1024
BATCH = 16
SEQ = 2048
DEPTH = 4

GRID_W = 64
CTX_LEN = 256
N_MIXERS = 3
EPS = 1e-6
NEG_INF = -1e30

NA_HEADS = 16
NA_HEAD_DIM = D_MODEL // NA_HEADS
NA_WIN_ROWS = 8
NA_WIN_COLS = 16
NA_QBLOCK_W = 16
NA_KBLOCK_W = 2 * NA_WIN_COLS

GMLP_CHUNK = 128
GMLP_WIDTH = D_MODEL
GMLP_GROUPS = 8

POOL_WINDOWS = (2, 4, 8, 16)
POOL_GROUP = D_MODEL // len(POOL_WINDOWS)

FFN_HIDDEN = -(-8 * D_MODEL // (3 * 256)) * 256

N_A = len(range(0, DEPTH, N_MIXERS))
N_B = len(range(1, DEPTH, N_MIXERS))
N_C = len(range(2, DEPTH, N_MIXERS))

kernel_name = 'hybrid_na_gmlp_pool_prefix_dit'


def rmsnorm(x, g):
    xf = x.astype(jnp.float32)
    y = xf * lax.rsqrt(jnp.mean(xf * xf, axis=-1, keepdims=True) + EPS)
    return (y * g.astype(jnp.float32)).astype(x.dtype)


def layernorm(x, g, b):
    xf = x.astype(jnp.float32)
    mu = jnp.mean(xf, axis=-1, keepdims=True)
    xc = xf - mu
    var = jnp.mean(xc * xc, axis=-1, keepdims=True)
    return (xc * lax.rsqrt(var + EPS) * g.astype(jnp.float32) + b.astype(jnp.float32)).astype(x.dtype)


def swiglu(h, w_gate, w_up, w_down):
    return (jax.nn.silu(h @ w_gate) * (h @ w_up)) @ w_down


def _heads(t):
    B, L, _ = t.shape
    return t.reshape(B, L, NA_HEADS, NA_HEAD_DIM).transpose(0, 2, 1, 3)


def _na_column_tables():
    n_cb = GRID_W // NA_QBLOCK_W
    qcol = np.arange(GRID_W).reshape(n_cb, NA_QBLOCK_W)
    kc0 = np.clip(np.arange(n_cb) * NA_QBLOCK_W - NA_WIN_COLS // 2, 0, GRID_W - NA_KBLOCK_W)
    kcol = kc0[:, None] + np.arange(NA_KBLOCK_W)
    cstart = np.clip(qcol - NA_WIN_COLS // 2, 0, GRID_W - NA_WIN_COLS)
    col_ok = (kcol[:, None, :] >= cstart[..., None]) & (kcol[:, None, :] < cstart[..., None] + NA_WIN_COLS)
    dc_idx = np.clip(kcol[:, None, :] - qcol[:, :, None] + NA_WIN_COLS - 1, 0, 2 * NA_WIN_COLS - 2)
    return kcol, col_ok, dc_idx


def na_mixer(h, hc, w_qkv, w_o, q_g, k_g, rpb, ctx_out):
    B, S, _ = h.shape
    rows = S // GRID_W
    rw = min(NA_WIN_ROWS, rows)
    n_cb = GRID_W // NA_QBLOCK_W
    H, hd, QB, KB = NA_HEADS, NA_HEAD_DIM, NA_QBLOCK_W, NA_KBLOCK_W
    scale = hd ** -0.5
    kcol, col_ok, dc_idx = _na_column_tables()

    q, k, v = jnp.split(h @ w_qkv, 3, axis=-1)
    q, k, v = rmsnorm(_heads(q), q_g), rmsnorm(_heads(k), k_g), _heads(v)
    qc, kc, vc = jnp.split(hc @ w_qkv, 3, axis=-1)
    kc, vc = rmsnorm(_heads(kc), k_g), _heads(vc)

    kgrid = k.reshape(B, H, rows, GRID_W, hd)
    vgrid = v.reshape(B, H, rows, GRID_W, hd)
    q_rows = jnp.moveaxis(q.reshape(B, H, rows, GRID_W, hd), 2, 0)
    rpb_c = rpb[:, :, dc_idx]
    nloc = rw * KB
    ok = np.broadcast_to(col_ok[:, :, None, :], (n_cb, QB, rw, KB)).reshape(n_cb, QB, nloc)

    def row_block(args):
        q_r, r = args
        rs = jnp.clip(r - rw // 2, 0, rows - rw)

        def gather(t):
            t = lax.dynamic_slice_in_dim(t, rs, rw, axis=2)
            t = jnp.take(t, kcol, axis=3)
            return t.transpose(0, 1, 3, 2, 4, 5).reshape(B, H, n_cb, nloc, hd)

        kb, vb = gather(kgrid), gather(vgrid)
        qb = q_r.reshape(B, H, n_cb, QB, hd)
        dr = rs + jnp.arange(rw) - r + NA_WIN_ROWS - 1
        bias = jnp.take(rpb_c, dr, axis=1).transpose(0, 2, 3, 1, 4).reshape(H, n_cb, QB, nloc)
        s_loc = jnp.einsum('bhjqd,bhjkd->bhjqk', qb, kb).astype(jnp.float32) * scale + bias.astype(jnp.float32)
        s_loc = jnp.where(ok, s_loc, NEG_INF)
        s_ctx = jnp.einsum('bhjqd,bhkd->bhjqk', qb, kc).astype(jnp.float32) * scale
        p = jax.nn.softmax(jnp.concatenate([s_loc, s_ctx], axis=-1), axis=-1).astype(v.dtype)
        o = (jnp.einsum('bhjqk,bhjkd->bhjqd', p[..., :nloc], vb)
             + jnp.einsum('bhjqk,bhkd->bhjqd', p[..., nloc:], vc))
        return o.reshape(B, H, GRID_W, hd)

    o = lax.map(row_block, (q_rows, jnp.arange(rows)))
    y = o.transpose(1, 0, 3, 2, 4).reshape(B, S, H * hd) @ w_o
    if not ctx_out:
        return y, None
    qc = rmsnorm(_heads(qc), q_g)
    sc = jnp.einsum('bhqd,bhkd->bhqk', qc, kc).astype(jnp.float32) * scale
    pc = jax.nn.softmax(sc, axis=-1).astype(vc.dtype)
    oc = jnp.einsum('bhqk,bhkd->bhqd', pc, vc)
    Bc, _, Lc, _ = oc.shape
    yc = oc.transpose(0, 2, 1, 3).reshape(Bc, Lc, H * hd) @ w_o
    return y, yc


def gmlp_mixer(h, w_in, b_in, ln_g, ln_b, w_s, b_s, w_out):
    B, L, _ = h.shape
    z = jax.nn.gelu(h @ w_in + b_in)
    u, v = jnp.split(z, 2, axis=-1)
    v = layernorm(v, ln_g, ln_b)
    vg = v.reshape(B, L // GMLP_CHUNK, GMLP_CHUNK, GMLP_GROUPS, GMLP_WIDTH // GMLP_GROUPS)
    mixed = jnp.einsum('gpq,bnqgd->bnpgd', w_s, vg) + b_s.T[None, None, :, :, None]
    return (u * mixed.reshape(B, L, GMLP_WIDTH)) @ w_out


def pool_mixer(h, w_pool, pool_scale):
    B, L, _ = h.shape
    t = np.arange(L)
    outs = []
    for g, w in enumerate(POOL_WINDOWS):
        hg = h[..., g * POOL_GROUP:(g + 1) * POOL_GROUP].astype(jnp.float32)
        cs = jnp.pad(jnp.cumsum(hg, axis=1), ((0, 0), (1, 0), (0, 0)))
        lo = np.clip(t - w // 2, 0, L)
        hi = np.clip(t - w // 2 + w, 0, L)
        cnt = (hi - lo).astype(np.float32)[:, None]
        mean = (jnp.take(cs, hi, axis=1) - jnp.take(cs, lo, axis=1)) / cnt
        outs.append(jnp.einsum('bld,de->ble', (mean - hg).astype(h.dtype), w_pool[g]))
    return jnp.concatenate(outs, axis=-1) * pool_scale


def setup_inputs(seed: int = 0) -> dict:
    key = jax.random.key(seed)
    ks = jax.random.split(key, 26)
    D, F, H, hd = D_MODEL, FFN_HIDDEN, NA_HEADS, NA_HEAD_DIM

    def nrm(k, shape, s):
        return jax.random.normal(k, shape, jnp.float32) * s

    return {
        'x': nrm(ks[0], (BATCH, SEQ, D), 1.0),
        'c': nrm(ks[1], (BATCH, D), 1.0),
        'ctx': nrm(ks[2], (BATCH, CTX_LEN, D), 1.0),
        'c_ctx': nrm(ks[3], (D,), 1.0),
        'ada_w': nrm(ks[4], (DEPTH, D, 6 * D), 0.5 * D ** -0.5),
        'ada_b': nrm(ks[5], (DEPTH, 6 * D), 0.02),
        'norm1_g': 1.0 + nrm(ks[6], (DEPTH, D), 0.02),
        'norm2_g': 1.0 + nrm(ks[7], (DEPTH, D), 0.02),
        'ffn_w_gate': nrm(ks[8], (DEPTH, D, F), D ** -0.5),
        'ffn_w_up': nrm(ks[9], (DEPTH, D, F), D ** -0.5),
        'ffn_w_down': nrm(ks[10], (DEPTH, F, D), F ** -0.5),
        'na_w_qkv': nrm(ks[11], (N_A, D, 3 * D), D ** -0.5),
        'na_w_o': nrm(ks[12], (N_A, D, D), D ** -0.5),
        'na_q_norm': 1.0 + nrm(ks[13], (N_A, hd), 0.02),
        'na_k_norm': 1.0 + nrm(ks[14], (N_A, hd), 0.02),
        'na_rpb': nrm(ks[15], (N_A, H, 2 * NA_WIN_ROWS - 1, 2 * NA_WIN_COLS - 1), 0.1),
        'gm_w_in': nrm(ks[16], (N_B, D, 2 * GMLP_WIDTH), D ** -0.5),
        'gm_b_in': nrm(ks[17], (N_B, 2 * GMLP_WIDTH), 0.02),
        'gm_ln_g': 1.0 + nrm(ks[18], (N_B, GMLP_WIDTH), 0.02),
        'gm_ln_b': nrm(ks[19], (N_B, GMLP_WIDTH), 0.02),
        'gm_w_s': nrm(ks[20], (N_B, GMLP_GROUPS, GMLP_CHUNK, GMLP_CHUNK), GMLP_CHUNK ** -0.5),
        'gm_b_s': 1.0 + nrm(ks[21], (N_B, GMLP_GROUPS, GMLP_CHUNK), 0.02),
        'gm_w_out': nrm(ks[22], (N_B, GMLP_WIDTH, D), GMLP_WIDTH ** -0.5),
        'pool_w': nrm(ks[23], (N_C, len(POOL_WINDOWS), POOL_GROUP, POOL_GROUP), POOL_GROUP ** -0.5),
        'pool_scale': 1.0 + nrm(ks[24], (N_C, D), 0.02),
    }


def reference(x, c, ctx, c_ctx, ada_w, ada_b, norm1_g, norm2_g, ffn_w_gate, ffn_w_up, ffn_w_down,
              na_w_qkv, na_w_o, na_q_norm, na_k_norm, na_rpb,
              gm_w_in, gm_b_in, gm_ln_g, gm_ln_b, gm_w_s, gm_b_s, gm_w_out,
              pool_w, pool_scale):
    s_lat = jax.nn.silu(c)
    s_ctx = jax.nn.silu(c_ctx)
    xc = ctx
    for i in range(DEPTH):
        kind, j = i % N_MIXERS, i // N_MIXERS
        last = i == DEPTH - 1
        need_ctx = (not last) or kind == 0
        sh1, sc1, g1, sh2, sc2, g2 = jnp.split((s_lat @ ada_w[i] + ada_b[i])[:, None, :], 6, axis=-1)
        csh1, csc1, cg1, csh2, csc2, cg2 = jnp.split(s_ctx @ ada_w[i] + ada_b[i], 6, axis=-1)

        h = rmsnorm(x, norm1_g[i]) * (1.0 + sc1) + sh1
        hc = rmsnorm(xc, norm1_g[i]) * (1.0 + csc1) + csh1 if need_ctx else None
        if kind == 0:
            y, yc = na_mixer(h, hc, na_w_qkv[j], na_w_o[j], na_q_norm[j], na_k_norm[j], na_rpb[j],
                             ctx_out=not last)
        elif kind == 1:
            gp = (gm_w_in[j], gm_b_in[j], gm_ln_g[j], gm_ln_b[j], gm_w_s[j], gm_b_s[j], gm_w_out[j])
            y = gmlp_mixer(h, *gp)
            yc = gmlp_mixer(hc, *gp) if not last else None
        else:
            y = pool_mixer(h, pool_w[j], pool_scale[j])
            yc = pool_mixer(hc, pool_w[j], pool_scale[j]) if not last else None

        x = x + g1 * y
        h2 = rmsnorm(x, norm2_g[i]) * (1.0 + sc2) + sh2
        x = x + g2 * swiglu(h2, ffn_w_gate[i], ffn_w_up[i], ffn_w_down[i])
        if not last:
            xc = xc + cg1 * yc
            hc2 = rmsnorm(xc, norm2_g[i]) * (1.0 + csc2) + csh2
            xc = xc + cg2 * swiglu(hc2, ffn_w_gate[i], ffn_w_up[i], ffn_w_down[i])
    return x
```

```python
import functools

import numpy as np
import jax
import jax.numpy as jnp
from jax import lax
from jax.experimental import pallas as pl
from jax.experimental.pallas import tpu as pltpu

D_MODEL = 1024
DEPTH = 4
GRID_W = 64
N_MIXERS = 3
EPS = 1e-6
NEG_INF = -1e30

NA_HEADS = 16
NA_HEAD_DIM = D_MODEL // NA_HEADS
NA_WIN_ROWS = 8
NA_WIN_COLS = 16
HEADS_PER_BLOCK = 2
PAIR_W = HEADS_PER_BLOCK * NA_HEAD_DIM
N_PAIRS = NA_HEADS // HEADS_PER_BLOCK
Q_TILE_ROWS = 4
Q_TILE = Q_TILE_ROWS * GRID_W
N_DR0 = 2 * NA_WIN_ROWS - 2

GMLP_CHUNK = 128
GMLP_GROUPS = 8
POOL_WINDOWS = (2, 4, 8, 16)
POOL_GROUP = D_MODEL // len(POOL_WINDOWS)
POOL_HALO = 8

MOD_ROWS = 32
ADA_TN = 1024
TM = 512
POOL_TM = 256
VMEM_LIMIT_BYTES = 56 * 1024 * 1024

BF16 = jnp.bfloat16
F32 = jnp.float32


def _dot(a, b):
    return jnp.dot(a, b, preferred_element_type=F32)


def _dot_nt(a, b):
    return lax.dot_general(a, b, (((1,), (1,)), ((), ())), preferred_element_type=F32)


def _silu(x):
    return x / (1.0 + jnp.exp(-x))


def _const_spec(shape):
    nd = len(shape)
    return pl.BlockSpec(shape, lambda *_: (0,) * nd, pipeline_mode=pl.Buffered(1))


def _params(n_axes=1):
    return pltpu.CompilerParams(
        dimension_semantics=("arbitrary",) * n_axes, vmem_limit_bytes=VMEM_LIMIT_BYTES)


def _norm_mod(x, g, sc, sh):
    ms = jnp.mean(x * x, axis=-1, keepdims=True)
    return (x * lax.rsqrt(ms + EPS) * g) * (1.0 + sc) + sh


def _ada_kernel(c_ref, w_ref, b_ref, o_ref):
    s = _silu(c_ref[...]).astype(BF16)
    o_ref[0] = _dot(s, w_ref[0].astype(BF16)) + b_ref[0]


def _ada_mods(c_all, ada_w, ada_b):
    n = ada_w.shape[-1]
    return pl.pallas_call(
        _ada_kernel,
        grid=(DEPTH, n // ADA_TN),
        in_specs=[
            pl.BlockSpec((MOD_ROWS, D_MODEL), lambda i, j: (0, 0)),
            pl.BlockSpec((1, D_MODEL, ADA_TN), lambda i, j: (i, 0, j)),
            pl.BlockSpec((1, 1, ADA_TN), lambda i, j: (i, 0, j)),
        ],
        out_specs=pl.BlockSpec((1, MOD_ROWS, ADA_TN), lambda i, j: (i, 0, j)),
        out_shape=jax.ShapeDtypeStruct((DEPTH, MOD_ROWS, n), F32),
        compiler_params=_params(2),
        name="ada_mods",
    )(c_all, ada_w, ada_b.reshape(DEPTH, 1, n))


def _post_kernel(pooled, a_ref, x_ref, mod_ref, n2g_ref, *rest):
    if pooled:
        wp_ref, ps_ref, wg_ref, wu_ref, wd_ref, o_ref = rest
        a = a_ref[...]
        y = jnp.concatenate(
            [_dot(a[:, g * POOL_GROUP:(g + 1) * POOL_GROUP], wp_ref[g])
             for g in range(len(POOL_WINDOWS))], axis=-1) * ps_ref[...]
    else:
        wm_ref, wg_ref, wu_ref, wd_ref, o_ref = rest
        y = _dot(a_ref[...], wm_ref[...])
    mod = mod_ref[0]
    g1, sh2, sc2, g2 = mod[2:3], mod[3:4], mod[4:5], mod[5:6]
    x1 = x_ref[...] + g1 * y
    h2 = _norm_mod(x1, n2g_ref[...], sc2, sh2).astype(BF16)
    act = (_silu(_dot(h2, wg_ref[...])) * _dot(h2, wu_ref[...])).astype(BF16)
    o_ref[...] = x1 + g2 * _dot(act, wd_ref[...])


def _post(a, x, mod, rows_per_mod, n2g, mix_w, pool_scale, wg, wu, wd):
    t = x.shape[0]
    pooled = pool_scale is not None
    tok = lambda i: (i, 0)
    in_specs = [
        pl.BlockSpec((TM, D_MODEL), tok),
        pl.BlockSpec((TM, D_MODEL), tok),
        pl.BlockSpec((1, 6, D_MODEL), lambda i: (i * TM // rows_per_mod, 0, 0)),
        _const_spec((1, D_MODEL)),
        _const_spec(mix_w.shape),
    ]
    args = [a, x, mod, n2g, mix_w]
    if pooled:
        in_specs.append(_const_spec((1, D_MODEL)))
        args.append(pool_scale)
    in_specs += [_const_spec(wg.shape), _const_spec(wu.shape), _const_spec(wd.shape)]
    args += [wg, wu, wd]
    return pl.pallas_call(
        functools.partial(_post_kernel, pooled),
        grid=(t // TM,),
        in_specs=in_specs,
        out_specs=pl.BlockSpec((TM, D_MODEL), tok),
        out_shape=jax.ShapeDtypeStruct((t, D_MODEL), F32),
        compiler_params=_params(),
        name="post_pool" if pooled else "post_dense",
    )(*args)


def _qkv_kernel(x_ref, mod_ref, n1g_ref, w_ref, qg_ref, kg_ref, seg_ref, q_ref, k_ref, v_ref):
    mod = mod_ref[0]
    h = _norm_mod(x_ref[...], n1g_ref[...], mod[1:2], mod[0:1]).astype(BF16)
    qkv = _dot(h, w_ref[...])
    seg = seg_ref[...]
    seg_w = seg.shape[0]

    def head_norm(t, gain):
        ms = jnp.concatenate(
            [_dot((t[:, c:c + seg_w] * t[:, c:c + seg_w]).astype(BF16), seg)
             for c in range(0, D_MODEL, seg_w)], axis=-1) * (1.0 / NA_HEAD_DIM)
        return t * lax.rsqrt(ms + EPS) * gain

    q = head_norm(qkv[:, :D_MODEL], qg_ref[...]) * (NA_HEAD_DIM ** -0.5)
    k = head_norm(qkv[:, D_MODEL:2 * D_MODEL], kg_ref[...])
    q_ref[...] = q.astype(BF16)
    k_ref[...] = k.astype(BF16)
    v_ref[...] = qkv[:, 2 * D_MODEL:].astype(BF16)


def _qkv(x, mod, rows_per_mod, n1g, w_qkv, qg, kg, seg):
    t = x.shape[0]
    tok = lambda i: (i, 0)
    out = jax.ShapeDtypeStruct((t, D_MODEL), BF16)
    return pl.pallas_call(
        _qkv_kernel,
        grid=(t // TM,),
        in_specs=[
            pl.BlockSpec((TM, D_MODEL), tok),
            pl.BlockSpec((1, 6, D_MODEL), lambda i: (i * TM // rows_per_mod, 0, 0)),
            _const_spec((1, D_MODEL)),
            _const_spec(w_qkv.shape),
            _const_spec((1, D_MODEL)),
            _const_spec((1, D_MODEL)),
            _const_spec(seg.shape),
        ],
        out_specs=[pl.BlockSpec((TM, D_MODEL), tok)] * 3,
        out_shape=[out, out, out],
        compiler_params=_params(),
        name="qkv",
    )(x, mod, n1g, w_qkv, qg, kg, seg)


def _softmax_rows(s_parts):
    m = functools.reduce(jnp.maximum, [jnp.max(s, axis=-1, keepdims=True) for s in s_parts])
    p_parts = [jnp.exp(s - m) for s in s_parts]
    l = functools.reduce(jnp.add, [jnp.sum(p, axis=-1, keepdims=True) for p in p_parts])
    return p_parts, l


def _stack_heads(qt, first):
    zero = jnp.zeros_like(qt)
    return jnp.concatenate([jnp.where(first, qt, zero), jnp.where(first, zero, qt)], axis=0)


def _na_kernel(q_ref, k_ref, v_ref, kc_ref, vc_ref, tab_int_ref, tab_all_ref, o_ref):
    seq = q_ref.shape[1]
    n_tiles = seq // Q_TILE
    first = lax.broadcasted_iota(jnp.int32, (1, PAIR_W), 1) < NA_HEAD_DIM
    kc = kc_ref[0]
    vc = vc_ref[0]

    def tile(q0, k0, win_rows, tab_ref, dr_base):
        nk = win_rows * GRID_W
        qs = _stack_heads(q_ref[0, pl.ds(q0, Q_TILE), :], first)
        kw = k_ref[0, pl.ds(k0, nk), :]
        vw = v_ref[0, pl.ds(k0, nk), :]
        s_loc = _dot_nt(qs, kw)
        s_ctx = _dot_nt(qs, kc)
        p_loc, p_ctx, inv_l = [], [], []
        for hh in range(HEADS_PER_BLOCK):
            for i in range(Q_TILE_ROWS):
                r0 = hh * Q_TILE + i * GRID_W
                bias = jnp.concatenate(
                    [tab_ref[hh, 2 * wp - i + dr_base] for wp in range(win_rows // 2)], axis=-1)
                (pl_, pc_), l = _softmax_rows(
                    [s_loc[r0:r0 + GRID_W] + bias, s_ctx[r0:r0 + GRID_W]])
                p_loc.append(pl_.astype(BF16))
                p_ctx.append(pc_.astype(BF16))
                inv_l.append(1.0 / l)
        o = _dot(jnp.concatenate(p_loc, axis=0), vw) + _dot(jnp.concatenate(p_ctx, axis=0), vc)
        o = o * jnp.concatenate(inv_l, axis=0)
        o_ref[0, pl.ds(q0, Q_TILE), :] = jnp.where(first, o[:Q_TILE], o[Q_TILE:]).astype(o_ref.dtype)

    tile(0, 0, NA_WIN_ROWS, tab_all_ref, NA_WIN_ROWS - 1)
    tile(seq - Q_TILE, seq - NA_WIN_ROWS * GRID_W, NA_WIN_ROWS, tab_all_ref,
         NA_WIN_ROWS - 1 - Q_TILE_ROWS)

    def body(t, carry):
        q0 = pl.multiple_of(t * Q_TILE, Q_TILE)
        k0 = pl.multiple_of((t - 1) * Q_TILE, Q_TILE)
        tile(q0, k0, NA_WIN_ROWS + Q_TILE_ROWS, tab_int_ref, NA_WIN_ROWS - 1 - Q_TILE_ROWS)
        return carry

    lax.fori_loop(1, n_tiles - 1, body, 0)


def _na_attention(q, k, v, kc, vc, tab_int, tab_all):
    b, seq, _ = q.shape
    ctx_len = kc.shape[1]
    lat = pl.BlockSpec((1, seq, PAIR_W), lambda i, j: (i, 0, j))
    ctx = pl.BlockSpec((1, ctx_len, PAIR_W), lambda i, j: (i, 0, j))
    tab = pl.BlockSpec((HEADS_PER_BLOCK, N_DR0, GRID_W, PAIR_W), lambda i, j: (j, 0, 0, 0))
    return pl.pallas_call(
        _na_kernel,
        grid=(b, N_PAIRS),
        in_specs=[lat, lat, lat, ctx, ctx, tab, tab],
        out_specs=lat,
        out_shape=jax.ShapeDtypeStruct((b, seq, D_MODEL), BF16),
        compiler_params=_params(2),
        name="na_attention",
    )(q, k, v, kc, vc, tab_int, tab_all)


def _ctx_attn_kernel(q_ref, k_ref, v_ref, o_ref):
    n = q_ref.shape[1]
    first = lax.broadcasted_iota(jnp.int32, (1, PAIR_W), 1) < NA_HEAD_DIM
    qs = _stack_heads(q_ref[0], first)
    (p,), l = _softmax_rows([_dot_nt(qs, k_ref[0])])
    o = _dot(p.astype(BF16), v_ref[0]) * (1.0 / l)
    o_ref[0] = jnp.where(first, o[:n], o[n:]).astype(o_ref.dtype)


def _ctx_attention(q, k, v):
    b, n, _ = q.shape
    spec = pl.BlockSpec((1, n, PAIR_W), lambda i, j: (i, 0, j))
    return pl.pallas_call(
        _ctx_attn_kernel,
        grid=(b, N_PAIRS),
        in_specs=[spec, spec, spec],
        out_specs=spec,
        out_shape=jax.ShapeDtypeStruct((b, n, D_MODEL), BF16),
        compiler_params=_params(2),
        name="ctx_attention",
    )(q, k, v)


def _bias_tables(rpb):
    qcol = np.arange(GRID_W)[:, None]
    kcol = np.arange(GRID_W)[None, :]
    dc = np.clip(kcol - qcol + NA_WIN_COLS - 1, 0, 2 * NA_WIN_COLS - 2)
    cstart = np.clip(qcol - NA_WIN_COLS // 2, 0, GRID_W - NA_WIN_COLS)
    col_ok = (kcol >= cstart) & (kcol < cstart + NA_WIN_COLS)
    dr = np.arange(N_DR0)[:, None] + np.arange(2)[None, :]
    vals = rpb[:, dr][..., dc]
    row_ok = (dr >= NA_WIN_ROWS - 1 - NA_WIN_ROWS // 2) & (dr < 2 * NA_WIN_ROWS - 1 - NA_WIN_ROWS // 2)
    ok_all = np.broadcast_to(col_ok[None, None], (N_DR0, 2, GRID_W, GRID_W))
    ok_int = ok_all & row_ok[:, :, None, None]

    def table(ok):
        t = jnp.where(ok[None], vals.astype(F32), NEG_INF)
        return t.transpose(0, 1, 3, 2, 4).reshape(NA_HEADS, N_DR0, GRID_W, PAIR_W)

    return table(ok_int), table(ok_all)


def _gelu_tanh(x):
    return 0.5 * x * (1.0 + jnp.tanh(np.sqrt(2.0 / np.pi).astype(np.float32) * (x + 0.044715 * (x * x * x))))


def _gmlp_kernel(x_ref, mod_ref, n1g_ref, win_ref, bin_ref, lng_ref, lnb_ref, ws_ref, bs_ref, a_ref):
    mod = mod_ref[0]
    h = _norm_mod(x_ref[...], n1g_ref[...], mod[1:2], mod[0:1]).astype(BF16)
    z = _gelu_tanh(_dot(h, win_ref[...]) + bin_ref[...])
    u = z[:, :D_MODEL]
    v = z[:, D_MODEL:]
    mu = jnp.mean(v, axis=-1, keepdims=True)
    vc = v - mu
    var = jnp.mean(vc * vc, axis=-1, keepdims=True)
    vn = (vc * lax.rsqrt(var + EPS) * lng_ref[...] + lnb_ref[...]).astype(BF16)
    gw = D_MODEL // GMLP_GROUPS
    for n in range(x_ref.shape[0] // GMLP_CHUNK):
        rows = slice(n * GMLP_CHUNK, (n + 1) * GMLP_CHUNK)
        for g in range(GMLP_GROUPS):
            cols = slice(g * gw, (g + 1) * gw)
            mixed = _dot(ws_ref[g], vn[rows, cols]) + bs_ref[g]
            a_ref[rows, cols] = (u[rows, cols] * mixed).astype(a_ref.dtype)


def _gmlp(x, mod, rows_per_mod, n1g, w_in, b_in, ln_g, ln_b, w_s, b_s):
    t = x.shape[0]
    tok = lambda i: (i, 0)
    return pl.pallas_call(
        _gmlp_kernel,
        grid=(t // TM,),
        in_specs=[
            pl.BlockSpec((TM, D_MODEL), tok),
            pl.BlockSpec((1, 6, D_MODEL), lambda i: (i * TM // rows_per_mod, 0, 0)),
            _const_spec((1, D_MODEL)),
            _const_spec(w_in.shape),
            _const_spec(b_in.shape),
            _const_spec((1, D_MODEL)),
            _const_spec((1, D_MODEL)),
            _const_spec(w_s.shape),
            _const_spec(b_s.shape),
        ],
        out_specs=pl.BlockSpec((TM, D_MODEL), tok),
        out_shape=jax.ShapeDtypeStruct((t, D_MODEL), BF16),
        compiler_params=_params(),
        name="gmlp",
    )(x, mod, n1g, w_in, b_in, ln_g, ln_b, w_s, b_s)


def _pool_kernel(seq_len, x_ref, xp_ref, xn_ref, mod_ref, n1g_ref, a_ref, hbuf):
    tm = x_ref.shape[0]
    tiles_per_seq = seq_len // tm
    pos = pl.program_id(0) % tiles_per_seq
    mod = mod_ref[0]
    n1g = n1g_ref[...]
    norm = lambda x: _norm_mod(x, n1g, mod[1:2], mod[0:1])
    zero = jnp.zeros((POOL_HALO, D_MODEL), F32)
    hbuf[0:POOL_HALO] = jnp.where(pos > 0, norm(xp_ref[...]), zero)
    hbuf[POOL_HALO:POOL_HALO + tm] = norm(x_ref[...])
    hbuf[POOL_HALO + tm:] = jnp.where(pos < tiles_per_seq - 1, norm(xn_ref[...]), zero)
    t = pos * tm + lax.broadcasted_iota(jnp.int32, (tm, 1), 0)
    for g, w in enumerate(POOL_WINDOWS):
        cols = slice(g * POOL_GROUP, (g + 1) * POOL_GROUP)
        total = functools.reduce(
            jnp.add, [hbuf[POOL_HALO + j:POOL_HALO + j + tm, cols] for j in range(-(w // 2), w - w // 2)])
        cnt = jnp.minimum(t - w // 2 + w, seq_len) - jnp.maximum(t - w // 2, 0)
        mean = total * (1.0 / cnt.astype(F32))
        a_ref[:, cols] = (mean - hbuf[POOL_HALO:POOL_HALO + tm, cols]).astype(a_ref.dtype)


def _pool(x, seq_len, mod, rows_per_mod, n1g):
    t = x.shape[0]
    tm = POOL_TM
    hb = tm // POOL_HALO
    n_halo = t // POOL_HALO
    return pl.pallas_call(
        functools.partial(_pool_kernel, seq_len),
        grid=(t // tm,),
        in_specs=[
            pl.BlockSpec((tm, D_MODEL), lambda i: (i, 0)),
            pl.BlockSpec((POOL_HALO, D_MODEL), lambda i: (jnp.maximum(i * hb - 1, 0), 0)),
            pl.BlockSpec((POOL_HALO, D_MODEL), lambda i: (jnp.minimum((i + 1) * hb, n_halo - 1), 0)),
            pl.BlockSpec((1, 6, D_MODEL), lambda i: (i * tm // rows_per_mod, 0, 0)),
            _const_spec((1, D_MODEL)),
        ],
        out_specs=pl.BlockSpec((tm, D_MODEL), lambda i: (i, 0)),
        out_shape=jax.ShapeDtypeStruct((t, D_MODEL), BF16),
        scratch_shapes=[pltpu.VMEM((tm + 2 * POOL_HALO, D_MODEL), F32)],
        compiler_params=_params(),
        name="pool",
    )(x, x, x, mod, n1g)


def kernel(x, c, ctx, c_ctx, ada_w, ada_b, norm1_g, norm2_g, ffn_w_gate, ffn_w_up, ffn_w_down, na_w_qkv, na_w_o, na_q_norm, na_k_norm, na_rpb, gm_w_in, gm_b_in, gm_ln_g, gm_ln_b, gm_w_s, gm_b_s, gm_w_out, pool_w, pool_scale):
    b, seq, d = x.shape
    ctx_len = ctx.shape[1]
    assert d == D_MODEL and seq % TM == 0 and (b * ctx_len) % TM == 0 and seq % (Q_TILE * 2) == 0

    c_all = jnp.zeros((MOD_ROWS, d), F32).at[:b].set(c).at[b].set(c_ctx)
    mods = _ada_mods(c_all, ada_w, ada_b)

    heads_per_seg = 4
    seg = jnp.asarray(np.kron(np.eye(heads_per_seg), np.ones((NA_HEAD_DIM, NA_HEAD_DIM))), BF16)
    row = lambda v: v.reshape(1, d)
    tile_gain = lambda g: jnp.tile(g, NA_HEADS).reshape(1, d)

    xl = x.reshape(b * seq, d)
    xc = ctx.reshape(b * ctx_len, d)
    for i in range(DEPTH):
        kind, j = i % N_MIXERS, i // N_MIXERS
        last = i == DEPTH - 1
        need_ctx = (not last) or kind == 0
        mod_l = mods[i, :b].reshape(b, 6, d)
        mod_c = mods[i, b:b + 1].reshape(1, 6, d)
        streams = [(xl, mod_l, seq, seq)]
        if need_ctx:
            streams.append((xc, mod_c, b * ctx_len, ctx_len))
        n1g, n2g = row(norm1_g[i]), row(norm2_g[i])
        wg, wu, wd = (w[i].astype(BF16) for w in (ffn_w_gate, ffn_w_up, ffn_w_down))

        pool_s = None
        if kind == 0:
            w_qkv = na_w_qkv[j].astype(BF16)
            qg, kg = tile_gain(na_q_norm[j]), tile_gain(na_k_norm[j])
            (ql, kl, vl), (qc, kc, vc) = (
                tuple(t.reshape(b, -1, d) for t in _qkv(xs, m, rpm, n1g, w_qkv, qg, kg, seg))
                for xs, m, rpm, _ in streams)
            tab_int, tab_all = _bias_tables(na_rpb[j])
            acts = [_na_attention(ql, kl, vl, kc, vc, tab_int, tab_all).reshape(b * seq, d)]
            if not last:
                acts.append(_ctx_attention(qc, kc, vc).reshape(b * ctx_len, d))
            mix_w = na_w_o[j].astype(BF16)
        elif kind == 1:
            w_in = gm_w_in[j].astype(BF16)
            b_in = gm_b_in[j].reshape(1, -1)
            w_s = gm_w_s[j].astype(BF16)
            b_s = jnp.broadcast_to(gm_b_s[j][:, :, None], (GMLP_GROUPS, GMLP_CHUNK, d // GMLP_GROUPS))
            acts = [_gmlp(xs, m, rpm, n1g, w_in, b_in, row(gm_ln_g[j]), row(gm_ln_b[j]), w_s, b_s)
                    for xs, m, rpm, _ in streams]
            mix_w = gm_w_out[j].astype(BF16)
        else:
            acts = [_pool(xs, sl, m, rpm, n1g) for xs, m, rpm, sl in streams]
            mix_w = pool_w[j].astype(BF16)
            pool_s = row(pool_scale[j])

        outs = [_post(a, xs, m, rpm, n2g, mix_w, pool_s, wg, wu, wd)
                for a, (xs, m, rpm, _) in zip(acts, streams)]
        xl = outs[0]
        if not last:
            xc = outs[1]
    return xl.reshape(b, seq, d)
```

```python
import functools

import numpy as np
import jax
import jax.numpy as jnp
from jax import lax
from jax.experimental import pallas as pl
from jax.experimental.pallas import tpu as pltpu

D_MODEL = 1024
DEPTH = 4
GRID_W = 64
N_MIXERS = 3
EPS = 1e-6
NEG_INF = -1e30

NA_HEADS = 16
NA_HEAD_DIM = D_MODEL // NA_HEADS
NA_WIN_ROWS = 8
NA_WIN_COLS = 16
HEADS_PER_BLOCK = 2
PAIR_W = HEADS_PER_BLOCK * NA_HEAD_DIM
N_PAIRS = NA_HEADS // HEADS_PER_BLOCK
Q_TILE_ROWS = 4
Q_TILE = Q_TILE_ROWS * GRID_W
N_DR0 = 2 * NA_WIN_ROWS - 2

GMLP_CHUNK = 128
GMLP_GROUPS = 8
POOL_WINDOWS = (2, 4, 8, 16)
POOL_GROUP = D_MODEL // len(POOL_WINDOWS)
POOL_HALO = 8

MOD_ROWS = 32
ADA_TN = 1024
TM = 512
POOL_TM = 256
VMEM_LIMIT_BYTES = 56 * 1024 * 1024

BF16 = jnp.bfloat16
F32 = jnp.float32


def _dot(a, b):
    return jnp.dot(a, b, preferred_element_type=F32)


def _dot_nt(a, b):
    return lax.dot_general(a, b, (((1,), (1,)), ((), ())), preferred_element_type=F32)


def _silu(x):
    return x / (1.0 + jnp.exp(-x))


def _const_spec(shape):
    nd = len(shape)
    return pl.BlockSpec(shape, lambda *_: (0,) * nd, pipeline_mode=pl.Buffered(1))


def _params(n_axes=1):
    return pltpu.CompilerParams(
        dimension_semantics=("arbitrary",) * n_axes, vmem_limit_bytes=VMEM_LIMIT_BYTES)


def _norm_mod(x, g, sc, sh):
    ms = jnp.mean(x * x, axis=-1, keepdims=True)
    return (x * lax.rsqrt(ms + EPS) * g) * (1.0 + sc) + sh


def _ada_kernel(c_ref, w_ref, b_ref, o_ref):
    s = _silu(c_ref[...]).astype(BF16)
    o_ref[0] = _dot(s, w_ref[0].astype(BF16)) + b_ref[0]


def _ada_mods(c_all, ada_w, ada_b):
    n = ada_w.shape[-1]
    return pl.pallas_call(
        _ada_kernel,
        grid=(DEPTH, n // ADA_TN),
        in_specs=[
            pl.BlockSpec((MOD_ROWS, D_MODEL), lambda i, j: (0, 0)),
            pl.BlockSpec((1, D_MODEL, ADA_TN), lambda i, j: (i, 0, j)),
            pl.BlockSpec((1, 1, ADA_TN), lambda i, j: (i, 0, j)),
        ],
        out_specs=pl.BlockSpec((1, MOD_ROWS, ADA_TN), lambda i, j: (i, 0, j)),
        out_shape=jax.ShapeDtypeStruct((DEPTH, MOD_ROWS, n), F32),
        compiler_params=_params(2),
        name="ada_mods",
    )(c_all, ada_w, ada_b.reshape(DEPTH, 1, n))


def _post_kernel(pooled, a_ref, x_ref, mod_ref, n2g_ref, *rest):
    if pooled:
        wp_ref, ps_ref, wg_ref, wu_ref, wd_ref, o_ref = rest
        a = a_ref[...]
        y = jnp.concatenate(
            [_dot(a[:, g * POOL_GROUP:(g + 1) * POOL_GROUP], wp_ref[g])
             for g in range(len(POOL_WINDOWS))], axis=-1) * ps_ref[...]
    else:
        wm_ref, wg_ref, wu_ref, wd_ref, o_ref = rest
        y = _dot(a_ref[...], wm_ref[...])
    mod = mod_ref[0]
    g1, sh2, sc2, g2 = mod[2:3], mod[3:4], mod[4:5], mod[5:6]
    x1 = x_ref[...] + g1 * y
    h2 = _norm_mod(x1, n2g_ref[...], sc2, sh2).astype(BF16)
    act = (_silu(_dot(h2, wg_ref[...])) * _dot(h2, wu_ref[...])).astype(BF16)
    o_ref[...] = x1 + g2 * _dot(act, wd_ref[...])


def _post(a, x, mod, rows_per_mod, n2g, mix_w, pool_scale, wg, wu, wd):
    t = x.shape[0]
    pooled = pool_scale is not None
    tok = lambda i: (i, 0)
    in_specs = [
        pl.BlockSpec((TM, D_MODEL), tok),
        pl.BlockSpec((TM, D_MODEL), tok),
        pl.BlockSpec((1, 6, D_MODEL), lambda i: (i * TM // rows_per_mod, 0, 0)),
        _const_spec((1, D_MODEL)),
        _const_spec(mix_w.shape),
    ]
    args = [a, x, mod, n2g, mix_w]
    if pooled:
        in_specs.append(_const_spec((1, D_MODEL)))
        args.append(pool_scale)
    in_specs += [_const_spec(wg.shape), _const_spec(wu.shape), _const_spec(wd.shape)]
    args += [wg, wu, wd]
    return pl.pallas_call(
        functools.partial(_post_kernel, pooled),
        grid=(t // TM,),
        in_specs=in_specs,
        out_specs=pl.BlockSpec((TM, D_MODEL), tok),
        out_shape=jax.ShapeDtypeStruct((t, D_MODEL), F32),
        compiler_params=_params(),
        name="post_pool" if pooled else "post_dense",
    )(*args)


def _qkv_kernel(x_ref, mod_ref, n1g_ref, w_ref, qg_ref, kg_ref, seg_ref, q_ref, k_ref, v_ref):
    mod = mod_ref[0]
    h = _norm_mod(x_ref[...], n1g_ref[...], mod[1:2], mod[0:1]).astype(BF16)
    qkv = _dot(h, w_ref[...])
    seg = seg_ref[...]
    seg_w = seg.shape[0]

    def head_norm(t, gain):
        ms = jnp.concatenate(
            [_dot((t[:, c:c + seg_w] * t[:, c:c + seg_w]).astype(BF16), seg)
             for c in range(0, D_MODEL, seg_w)], axis=-1) * (1.0 / NA_HEAD_DIM)
        return t * lax.rsqrt(ms + EPS) * gain

    q = head_norm(qkv[:, :D_MODEL], qg_ref[...]) * (NA_HEAD_DIM ** -0.5)
    k = head_norm(qkv[:, D_MODEL:2 * D_MODEL], kg_ref[...])
    q_ref[...] = q.astype(BF16)
    k_ref[...] = k.astype(BF16)
    v_ref[...] = qkv[:, 2 * D_MODEL:].astype(BF16)


def _qkv(x, mod, rows_per_mod, n1g, w_qkv, qg, kg, seg):
    t = x.shape[0]
    tok = lambda i: (i, 0)
    out = jax.ShapeDtypeStruct((t, D_MODEL), BF16)
    return pl.pallas_call(
        _qkv_kernel,
        grid=(t // TM,),
        in_specs=[
            pl.BlockSpec((TM, D_MODEL), tok),
            pl.BlockSpec((1, 6, D_MODEL), lambda i: (i * TM // rows_per_mod, 0, 0)),
            _const_spec((1, D_MODEL)),
            _const_spec(w_qkv.shape),
            _const_spec((1, D_MODEL)),
            _const_spec((1, D_MODEL)),
            _const_spec(seg.shape),
        ],
        out_specs=[pl.BlockSpec((TM, D_MODEL), tok)] * 3,
        out_shape=[out, out, out],
        compiler_params=_params(),
        name="qkv",
    )(x, mod, n1g, w_qkv, qg, kg, seg)


def _softmax_rows(s_parts):
    m = jnp.max(functools.reduce(jnp.maximum, s_parts), axis=-1, keepdims=True)
    p_parts = [jnp.exp(s - m) for s in s_parts]
    l = jnp.sum(functools.reduce(jnp.add, p_parts), axis=-1, keepdims=True)
    return p_parts, l


def _stack_heads(qt, first):
    zero = jnp.zeros_like(qt)
    return jnp.concatenate([jnp.where(first, qt, zero), jnp.where(first, zero, qt)], axis=0)


def _na_kernel(with_ctx_out, q_ref, k_ref, v_ref, *rest):
    if with_ctx_out:
        qc_ref, kc_ref, vc_ref, tab_int_ref, tab_all_ref, o_ref, oc_ref = rest
    else:
        kc_ref, vc_ref, tab_int_ref, tab_all_ref, o_ref = rest
    seq = q_ref.shape[1]
    n_tiles = seq // Q_TILE
    first = lax.broadcasted_iota(jnp.int32, (1, PAIR_W), 1) < NA_HEAD_DIM
    kc = kc_ref[0]
    vc = vc_ref[0]

    def tile(q0, k0, win_rows, tab_ref, dr_base, clamped):
        nk = win_rows * GRID_W
        qs = _stack_heads(q_ref[0, pl.ds(q0, Q_TILE), :], first)
        kw = k_ref[0, pl.ds(k0, nk), :]
        vw = v_ref[0, pl.ds(k0, nk), :]
        s_loc = _dot_nt(qs, kw)
        s_ctx = _dot_nt(qs, kc)
        p_loc, p_ctx, inv_l = [], [], []
        zero = jnp.zeros((GRID_W, PAIR_W), BF16)
        for hh in range(HEADS_PER_BLOCK):
            for i in range(Q_TILE_ROWS):
                r0 = hh * Q_TILE + i * GRID_W
                lo, hi = (0, win_rows) if clamped else (i, i + NA_WIN_ROWS)
                pairs = range(lo // 2, (hi + 1) // 2)
                parts = [s_loc[r0:r0 + GRID_W, wp * PAIR_W:(wp + 1) * PAIR_W]
                         + tab_ref[hh, 2 * wp - i + dr_base] for wp in pairs]
                n_loc = len(parts)
                parts += [s_ctx[r0:r0 + GRID_W, c:c + PAIR_W] for c in range(0, kc.shape[0], PAIR_W)]
                p_parts, l = _softmax_rows(parts)
                p_loc.append(jnp.concatenate(
                    [p_parts[pairs.index(wp)].astype(BF16) if wp in pairs else zero
                     for wp in range(win_rows // 2)], axis=-1))
                p_ctx.append(jnp.concatenate([p.astype(BF16) for p in p_parts[n_loc:]], axis=-1))
                inv_l.append(1.0 / l)
        o = _dot(jnp.concatenate(p_loc, axis=0), vw) + _dot(jnp.concatenate(p_ctx, axis=0), vc)
        o = o * jnp.concatenate(inv_l, axis=0)
        o_ref[0, pl.ds(q0, Q_TILE), :] = jnp.where(first, o[:Q_TILE], o[Q_TILE:]).astype(o_ref.dtype)

    if with_ctx_out:
        n = qc_ref.shape[1]
        (p,), l = _softmax_rows([_dot_nt(_stack_heads(qc_ref[0], first), kc)])
        oc = _dot(p.astype(BF16), vc) * (1.0 / l)
        oc_ref[0] = jnp.where(first, oc[:n], oc[n:]).astype(oc_ref.dtype)

    tile(0, 0, NA_WIN_ROWS, tab_all_ref, NA_WIN_ROWS - 1, True)
    tile(seq - Q_TILE, seq - NA_WIN_ROWS * GRID_W, NA_WIN_ROWS, tab_all_ref,
         Q_TILE_ROWS - 1, True)
    for t in range(1, n_tiles - 1):
        tile(t * Q_TILE, t * Q_TILE - (NA_WIN_ROWS // 2) * GRID_W, NA_WIN_ROWS + Q_TILE_ROWS,
             tab_int_ref, NA_WIN_ROWS - 1 - NA_WIN_ROWS // 2, False)


def _na_attention(q, k, v, qc, kc, vc, tab_int, tab_all):
    b, seq, _ = q.shape
    ctx_len = kc.shape[1]
    lat = pl.BlockSpec((1, seq, PAIR_W), lambda i, j: (i, 0, j))
    ctx = pl.BlockSpec((1, ctx_len, PAIR_W), lambda i, j: (i, 0, j))
    tab = pl.BlockSpec((HEADS_PER_BLOCK, N_DR0, GRID_W, PAIR_W), lambda i, j: (j, 0, 0, 0))
    with_ctx_out = qc is not None
    lat_shape = jax.ShapeDtypeStruct((b, seq, D_MODEL), BF16)
    ctx_shape = jax.ShapeDtypeStruct((b, ctx_len, D_MODEL), BF16)
    return pl.pallas_call(
        functools.partial(_na_kernel, with_ctx_out),
        grid=(b, N_PAIRS),
        in_specs=[lat, lat, lat] + [ctx] * (3 if with_ctx_out else 2) + [tab, tab],
        out_specs=[lat, ctx] if with_ctx_out else lat,
        out_shape=[lat_shape, ctx_shape] if with_ctx_out else lat_shape,
        compiler_params=_params(2),
        name="na_attention",
    )(q, k, v, *([qc] if with_ctx_out else []), kc, vc, tab_int, tab_all)


def _bias_tables(rpb):
    n_dc = 2 * NA_WIN_COLS - 1
    qcol = np.arange(GRID_W)[:, None]
    kcol = np.arange(GRID_W)[None, :]
    cstart = np.clip(qcol - NA_WIN_COLS // 2, 0, GRID_W - NA_WIN_COLS)
    col_ok = (kcol >= cstart) & (kcol < cstart + NA_WIN_COLS)
    dr = np.arange(N_DR0)[:, None, None] + np.arange(2)[None, None, :]
    row_ok = (dr >= NA_WIN_ROWS - 1 - NA_WIN_ROWS // 2) & (dr < 2 * NA_WIN_ROWS - 1 - NA_WIN_ROWS // 2)
    ok_all = np.broadcast_to(col_ok[None, :, None, :], (N_DR0, GRID_W, 2, GRID_W))
    ok_int = ok_all & row_ok[..., None]
    ring = jnp.concatenate([rpb[..., NA_WIN_COLS - 1:],
                            jnp.zeros(rpb.shape[:2] + (2 * GRID_W - n_dc,), rpb.dtype),
                            rpb[..., :NA_WIN_COLS - 1]], axis=-1)
    skew = jnp.tile(ring, (1, 1, GRID_W))[..., :GRID_W * (2 * GRID_W - 1)]
    toep = skew.reshape(rpb.shape[:2] + (GRID_W, 2 * GRID_W - 1))[..., :GRID_W]
    vals = jnp.concatenate([toep[:, :N_DR0], toep[:, 1:N_DR0 + 1]], axis=-1).astype(F32)

    def table(ok):
        return jnp.where(ok.reshape(1, N_DR0, GRID_W, PAIR_W), vals, NEG_INF)

    return table(ok_int), table(ok_all)


def _gelu_tanh(x):
    return 0.5 * x * (1.0 + jnp.tanh(np.sqrt(2.0 / np.pi).astype(np.float32) * (x + 0.044715 * (x * x * x))))


def _gmlp_kernel(x_ref, mod_ref, n1g_ref, win_ref, bin_ref, lng_ref, lnb_ref, ws_ref, bs_ref, a_ref):
    mod = mod_ref[0]
    h = _norm_mod(x_ref[...], n1g_ref[...], mod[1:2], mod[0:1]).astype(BF16)
    z = _gelu_tanh(_dot(h, win_ref[...]) + bin_ref[...])
    u = z[:, :D_MODEL]
    v = z[:, D_MODEL:]
    mu = jnp.mean(v, axis=-1, keepdims=True)
    vc = v - mu
    var = jnp.mean(vc * vc, axis=-1, keepdims=True)
    vn = (vc * lax.rsqrt(var + EPS) * lng_ref[...] + lnb_ref[...]).astype(BF16)
    gw = D_MODEL // GMLP_GROUPS
    for n in range(x_ref.shape[0] // GMLP_CHUNK):
        rows = slice(n * GMLP_CHUNK, (n + 1) * GMLP_CHUNK)
        for g in range(GMLP_GROUPS):
            cols = slice(g * gw, (g + 1) * gw)
            mixed = _dot(ws_ref[g], vn[rows, cols]) + bs_ref[g]
            a_ref[rows, cols] = (u[rows, cols] * mixed).astype(a_ref.dtype)


def _gmlp(x, mod, rows_per_mod, n1g, w_in, b_in, ln_g, ln_b, w_s, b_s):
    t = x.shape[0]
    tok = lambda i: (i, 0)
    return pl.pallas_call(
        _gmlp_kernel,
        grid=(t // TM,),
        in_specs=[
            pl.BlockSpec((TM, D_MODEL), tok),
            pl.BlockSpec((1, 6, D_MODEL), lambda i: (i * TM // rows_per_mod, 0, 0)),
            _const_spec((1, D_MODEL)),
            _const_spec(w_in.shape),
            _const_spec(b_in.shape),
            _const_spec((1, D_MODEL)),
            _const_spec((1, D_MODEL)),
            _const_spec(w_s.shape),
            _const_spec(b_s.shape),
        ],
        out_specs=pl.BlockSpec((TM, D_MODEL), tok),
        out_shape=jax.ShapeDtypeStruct((t, D_MODEL), BF16),
        compiler_params=_params(),
        name="gmlp",
    )(x, mod, n1g, w_in, b_in, ln_g, ln_b, w_s, b_s)


def _pool_kernel(seq_len, x_ref, xp_ref, xn_ref, mod_ref, n1g_ref, a_ref, hbuf):
    tm = x_ref.shape[0]
    tiles_per_seq = seq_len // tm
    pos = pl.program_id(0) % tiles_per_seq
    mod = mod_ref[0]
    n1g = n1g_ref[...]
    norm = lambda x: _norm_mod(x, n1g, mod[1:2], mod[0:1])
    zero = jnp.zeros((POOL_HALO, D_MODEL), F32)
    hbuf[0:POOL_HALO] = jnp.where(pos > 0, norm(xp_ref[...]), zero)
    hbuf[POOL_HALO:POOL_HALO + tm] = norm(x_ref[...])
    hbuf[POOL_HALO + tm:] = jnp.where(pos < tiles_per_seq - 1, norm(xn_ref[...]), zero)
    t = pos * tm + lax.broadcasted_iota(jnp.int32, (tm, 1), 0)
    for g, w in enumerate(POOL_WINDOWS):
        cols = slice(g * POOL_GROUP, (g + 1) * POOL_GROUP)
        total = functools.reduce(
            jnp.add, [hbuf[POOL_HALO + j:POOL_HALO + j + tm, cols] for j in range(-(w // 2), w - w // 2)])
        cnt = jnp.minimum(t - w // 2 + w, seq_len) - jnp.maximum(t - w // 2, 0)
        mean = total * (1.0 / cnt.astype(F32))
        a_ref[:, cols] = (mean - hbuf[POOL_HALO:POOL_HALO + tm, cols]).astype(a_ref.dtype)


def _pool(x, seq_len, mod, rows_per_mod, n1g):
    t = x.shape[0]
    tm = POOL_TM
    hb = tm // POOL_HALO
    n_halo = t // POOL_HALO
    return pl.pallas_call(
        functools.partial(_pool_kernel, seq_len),
        grid=(t // tm,),
        in_specs=[
            pl.BlockSpec((tm, D_MODEL), lambda i: (i, 0)),
            pl.BlockSpec((POOL_HALO, D_MODEL), lambda i: (jnp.maximum(i * hb - 1, 0), 0)),
            pl.BlockSpec((POOL_HALO, D_MODEL), lambda i: (jnp.minimum((i + 1) * hb, n_halo - 1), 0)),
            pl.BlockSpec((1, 6, D_MODEL), lambda i: (i * tm // rows_per_mod, 0, 0)),
            _const_spec((1, D_MODEL)),
        ],
        out_specs=pl.BlockSpec((tm, D_MODEL), lambda i: (i, 0)),
        out_shape=jax.ShapeDtypeStruct((t, D_MODEL), BF16),
        scratch_shapes=[pltpu.VMEM((tm + 2 * POOL_HALO, D_MODEL), F32)],
        compiler_params=_params(),
        name="pool",
    )(x, x, x, mod, n1g)


def kernel(x, c, ctx, c_ctx, ada_w, ada_b, norm1_g, norm2_g, ffn_w_gate, ffn_w_up, ffn_w_down, na_w_qkv, na_w_o, na_q_norm, na_k_norm, na_rpb, gm_w_in, gm_b_in, gm_ln_g, gm_ln_b, gm_w_s, gm_b_s, gm_w_out, pool_w, pool_scale):
    b, seq, d = x.shape
    ctx_len = ctx.shape[1]
    assert d == D_MODEL and seq % TM == 0 and (b * ctx_len) % TM == 0 and seq % (Q_TILE * 2) == 0

    c_all = jnp.zeros((MOD_ROWS, d), F32).at[:b].set(c).at[b].set(c_ctx)
    mods = _ada_mods(c_all, ada_w, ada_b)

    heads_per_seg = 4
    seg = jnp.asarray(np.kron(np.eye(heads_per_seg), np.ones((NA_HEAD_DIM, NA_HEAD_DIM))), BF16)
    row = lambda v: v.reshape(1, d)
    tile_gain = lambda g: jnp.tile(g, NA_HEADS).reshape(1, d)

    xl = x.reshape(b * seq, d)
    xc = ctx.reshape(b * ctx_len, d)
    for i in range(DEPTH):
        kind, j = i % N_MIXERS, i // N_MIXERS
        last = i == DEPTH - 1
        need_ctx = (not last) or kind == 0
        mod_l = mods[i, :b].reshape(b, 6, d)
        mod_c = mods[i, b:b + 1].reshape(1, 6, d)
        streams = [(xl, mod_l, seq, seq)]
        if need_ctx:
            streams.append((xc, mod_c, b * ctx_len, ctx_len))
        n1g, n2g = row(norm1_g[i]), row(norm2_g[i])
        wg, wu, wd = (w[i].astype(BF16) for w in (ffn_w_gate, ffn_w_up, ffn_w_down))

        pool_s = None
        if kind == 0:
            w_qkv = na_w_qkv[j].astype(BF16)
            qg, kg = tile_gain(na_q_norm[j]), tile_gain(na_k_norm[j])
            (ql, kl, vl), (qc, kc, vc) = (
                tuple(t.reshape(b, -1, d) for t in _qkv(xs, m, rpm, n1g, w_qkv, qg, kg, seg))
                for xs, m, rpm, _ in streams)
            tab_int, tab_all = _bias_tables(na_rpb[j])
            acts = _na_attention(ql, kl, vl, None if last else qc, kc, vc, tab_int, tab_all)
            acts = [a.reshape(-1, d) for a in (acts if isinstance(acts, (list, tuple)) else [acts])]
            mix_w = na_w_o[j].astype(BF16)
        elif kind == 1:
            w_in = gm_w_in[j].astype(BF16)
            b_in = gm_b_in[j].reshape(1, -1)
            w_s = gm_w_s[j].astype(BF16)
            b_s = jnp.broadcast_to(gm_b_s[j][:, :, None], (GMLP_GROUPS, GMLP_CHUNK, d // GMLP_GROUPS))
            acts = [_gmlp(xs, m, rpm, n1g, w_in, b_in, row(gm_ln_g[j]), row(gm_ln_b[j]), w_s, b_s)
                    for xs, m, rpm, _ in streams]
            mix_w = gm_w_out[j].astype(BF16)
        else:
            acts = [_pool(xs, sl, m, rpm, n1g) for xs, m, rpm, sl in streams]
            mix_w = pool_w[j].astype(BF16)
            pool_s = row(pool_scale[j])

        outs = [_post(a, xs, m, rpm, n2g, mix_w, pool_s, wg, wu, wd)
                for a, (xs, m, rpm, _) in zip(acts, streams)]
        xl = outs[0]
        if not last:
            xc = outs[1]
    return xl.reshape(b, seq, d)
```

```python
import functools

import numpy as np
import jax
import jax.numpy as jnp
from jax import lax
from jax.experimental import pallas as pl
from jax.experimental.pallas import tpu as pltpu

D_MODEL = 1024
DEPTH = 4
GRID_W = 64
N_MIXERS = 3
EPS = 1e-6
NEG_INF = -1e30

NA_HEADS = 16
NA_HEAD_DIM = D_MODEL // NA_HEADS
NA_WIN_ROWS = 8
NA_WIN_COLS = 16
HEADS_PER_BLOCK = 2
PAIR_W = HEADS_PER_BLOCK * NA_HEAD_DIM
N_PAIRS = NA_HEADS // HEADS_PER_BLOCK
Q_TILE_ROWS = 4
Q_TILE = Q_TILE_ROWS * GRID_W
N_DR0 = 2 * NA_WIN_ROWS - 2

GMLP_CHUNK = 128
GMLP_GROUPS = 8
POOL_WINDOWS = (2, 4, 8, 16)
POOL_GROUP = D_MODEL // len(POOL_WINDOWS)
POOL_HALO = 8
POOL_PAD = 16

MOD_ROWS = 32
ADA_TN = 1024
TM = 512
POOL_TM = 512
VMEM_LIMIT_BYTES = 56 * 1024 * 1024

BF16 = jnp.bfloat16
F32 = jnp.float32


def _dot(a, b):
    return jnp.dot(a, b, preferred_element_type=F32)


def _dot_nt(a, b):
    return lax.dot_general(a, b, (((1,), (1,)), ((), ())), preferred_element_type=F32)


def _silu(x):
    return x / (1.0 + jnp.exp(-x))


def _const_spec(shape):
    nd = len(shape)
    return pl.BlockSpec(shape, lambda *_: (0,) * nd, pipeline_mode=pl.Buffered(1))


def _params(n_axes=1):
    return pltpu.CompilerParams(
        dimension_semantics=("arbitrary",) * n_axes, vmem_limit_bytes=VMEM_LIMIT_BYTES)


def _norm_mod(x, g, sc, sh):
    ms = jnp.mean(x * x, axis=-1, keepdims=True)
    return (x * lax.rsqrt(ms + EPS) * g) * (1.0 + sc) + sh


def _ada_kernel(c_ref, w_ref, b_ref, o_ref):
    s = _silu(c_ref[...]).astype(BF16)
    o_ref[0] = _dot(s, w_ref[0].astype(BF16)) + b_ref[0]


def _ada_mods(c_all, ada_w, ada_b):
    n = ada_w.shape[-1]
    return pl.pallas_call(
        _ada_kernel,
        grid=(DEPTH, n // ADA_TN),
        in_specs=[
            pl.BlockSpec((MOD_ROWS, D_MODEL), lambda i, j: (0, 0)),
            pl.BlockSpec((1, D_MODEL, ADA_TN), lambda i, j: (i, 0, j)),
            pl.BlockSpec((1, 1, ADA_TN), lambda i, j: (i, 0, j)),
        ],
        out_specs=pl.BlockSpec((1, MOD_ROWS, ADA_TN), lambda i, j: (i, 0, j)),
        out_shape=jax.ShapeDtypeStruct((DEPTH, MOD_ROWS, n), F32),
        compiler_params=_params(2),
        name="ada_mods",
    )(c_all, ada_w, ada_b.reshape(DEPTH, 1, n))


def _post_kernel(pooled, a_ref, x_ref, mod_ref, n2g_ref, *rest):
    if pooled:
        wp_ref, ps_ref, wg_ref, wu_ref, wd_ref, o_ref = rest
        a = a_ref[...]
        y = jnp.concatenate(
            [_dot(a[:, g * POOL_GROUP:(g + 1) * POOL_GROUP], wp_ref[g])
             for g in range(len(POOL_WINDOWS))], axis=-1) * ps_ref[...]
    else:
        wm_ref, wg_ref, wu_ref, wd_ref, o_ref = rest
        y = _dot(a_ref[...], wm_ref[...])
    mod = mod_ref[0]
    g1, sh2, sc2, g2 = mod[2:3], mod[3:4], mod[4:5], mod[5:6]
    x1 = x_ref[...] + g1 * y
    h2 = _norm_mod(x1, n2g_ref[...], sc2, sh2).astype(BF16)
    act = (_silu(_dot(h2, wg_ref[...])) * _dot(h2, wu_ref[...])).astype(BF16)
    o_ref[...] = x1 + g2 * _dot(act, wd_ref[...])


def _post(a, x, mod, rows_per_mod, n2g, mix_w, pool_scale, wg, wu, wd):
    t = x.shape[0]
    pooled = pool_scale is not None
    tok = lambda i: (i, 0)
    in_specs = [
        pl.BlockSpec((TM, D_MODEL), tok),
        pl.BlockSpec((TM, D_MODEL), tok),
        pl.BlockSpec((1, 6, D_MODEL), lambda i: (i * TM // rows_per_mod, 0, 0)),
        _const_spec((1, D_MODEL)),
        _const_spec(mix_w.shape),
    ]
    args = [a, x, mod, n2g, mix_w]
    if pooled:
        in_specs.append(_const_spec((1, D_MODEL)))
        args.append(pool_scale)
    in_specs += [_const_spec(wg.shape), _const_spec(wu.shape), _const_spec(wd.shape)]
    args += [wg, wu, wd]
    return pl.pallas_call(
        functools.partial(_post_kernel, pooled),
        grid=(t // TM,),
        in_specs=in_specs,
        out_specs=pl.BlockSpec((TM, D_MODEL), tok),
        out_shape=jax.ShapeDtypeStruct((t, D_MODEL), F32),
        compiler_params=_params(),
        name="post_pool" if pooled else "post_dense",
    )(*args)


def _qkv_kernel(x_ref, mod_ref, n1g_ref, w_ref, qg_ref, kg_ref, seg_ref, q_ref, k_ref, vt_ref):
    mod = mod_ref[0]
    h = _norm_mod(x_ref[...], n1g_ref[...], mod[1:2], mod[0:1]).astype(BF16)
    qkv = _dot(h, w_ref[...])
    seg = seg_ref[...]
    seg_w = seg.shape[0]

    def head_norm(t, gain):
        ms = jnp.concatenate(
            [_dot((t[:, c:c + seg_w] * t[:, c:c + seg_w]).astype(BF16), seg)
             for c in range(0, D_MODEL, seg_w)], axis=-1) * (1.0 / NA_HEAD_DIM)
        return t * lax.rsqrt(ms + EPS) * gain

    q = head_norm(qkv[:, :D_MODEL], qg_ref[...]) * (NA_HEAD_DIM ** -0.5)
    k = head_norm(qkv[:, D_MODEL:2 * D_MODEL], kg_ref[...])
    q_ref[...] = q.astype(BF16)
    k_ref[...] = k.astype(BF16)
    vt_ref[0] = qkv[:, 2 * D_MODEL:].T.astype(BF16)


def _qkv(x, seq_len, tm, mod, rows_per_mod, n1g, w_qkv, qg, kg, seg):
    t = x.shape[0]
    tiles_per_seq = seq_len // tm
    tok = lambda i: (i, 0)
    out = jax.ShapeDtypeStruct((t, D_MODEL), BF16)
    return pl.pallas_call(
        _qkv_kernel,
        grid=(t // tm,),
        in_specs=[
            pl.BlockSpec((tm, D_MODEL), tok),
            pl.BlockSpec((1, 6, D_MODEL), lambda i: (i * tm // rows_per_mod, 0, 0)),
            _const_spec((1, D_MODEL)),
            _const_spec(w_qkv.shape),
            _const_spec((1, D_MODEL)),
            _const_spec((1, D_MODEL)),
            _const_spec(seg.shape),
        ],
        out_specs=[pl.BlockSpec((tm, D_MODEL), tok), pl.BlockSpec((tm, D_MODEL), tok),
                   pl.BlockSpec((1, D_MODEL, tm), lambda i: (i // tiles_per_seq, 0, i % tiles_per_seq))],
        out_shape=[out, out, jax.ShapeDtypeStruct((t // seq_len, D_MODEL, seq_len), BF16)],
        compiler_params=_params(),
        name="qkv",
    )(x, mod, n1g, w_qkv, qg, kg, seg)


def _softmax_cols(s_parts):
    m = jnp.max(functools.reduce(jnp.maximum, s_parts), axis=0, keepdims=True)
    p_parts = [jnp.exp(s - m) for s in s_parts]
    l = jnp.sum(functools.reduce(jnp.add, p_parts), axis=0, keepdims=True)
    return p_parts, l


def _stack_heads(qt, first):
    zero = jnp.zeros_like(qt)
    return jnp.concatenate([jnp.where(first, qt, zero), jnp.where(first, zero, qt)], axis=0)


def _merge_heads_t(o_t, first_rows):
    n = o_t.shape[1] // HEADS_PER_BLOCK
    return jnp.where(first_rows, o_t[:, :n], o_t[:, n:]).T


def _na_kernel(with_ctx_out, q_ref, k_ref, vt_ref, *rest):
    if with_ctx_out:
        qc_ref, kc_ref, vct_ref, tab_int_ref, tab_all_ref, o_ref, oc_ref = rest
    else:
        kc_ref, vct_ref, tab_int_ref, tab_all_ref, o_ref = rest
    seq = q_ref.shape[1]
    n_tiles = seq // Q_TILE
    first = lax.broadcasted_iota(jnp.int32, (1, PAIR_W), 1) < NA_HEAD_DIM
    first_rows = lax.broadcasted_iota(jnp.int32, (PAIR_W, 1), 0) < NA_HEAD_DIM
    kc = kc_ref[0]
    vct = vct_ref[0]
    n_ctx_parts = kc.shape[0] // GRID_W
    lanes_per_head = Q_TILE
    row_pairs = Q_TILE_ROWS // 2

    if with_ctx_out:
        (p,), l = _softmax_cols([_dot_nt(kc, _stack_heads(qc_ref[0], first))])
        oc_t = _dot(vct, p.astype(BF16)) * (1.0 / l)
        oc_ref[0] = _merge_heads_t(oc_t, first_rows).astype(oc_ref.dtype)

    def scores(q0, k0, win_rows, *_):
        qs = _stack_heads(q_ref[0, pl.ds(q0, Q_TILE), :], first)
        s_loc = _dot_nt(k_ref[0, pl.ds(k0, win_rows * GRID_W), :], qs)
        s_ctx = _dot_nt(kc, qs)
        return s_loc, s_ctx

    def finish(s_loc, s_ctx, q0, k0, win_rows, tab_ref, dr_base, clamped):
        nk = win_rows * GRID_W
        zero = jnp.zeros((GRID_W, PAIR_W), BF16)
        p_loc, p_ctx, inv_l = [], [], []
        for hh in range(HEADS_PER_BLOCK):
            for ip in range(row_pairs):
                lanes = slice(hh * lanes_per_head + ip * PAIR_W, hh * lanes_per_head + (ip + 1) * PAIR_W)
                rows = range(0, win_rows) if clamped else range(2 * ip, 2 * ip + NA_WIN_ROWS + 1)
                parts = [s_loc[wr * GRID_W:(wr + 1) * GRID_W, lanes] + tab_ref[hh, wr - 2 * ip + dr_base - 1]
                         for wr in rows]
                parts += [s_ctx[j * GRID_W:(j + 1) * GRID_W, lanes] for j in range(n_ctx_parts)]
                p_parts, l = _softmax_cols(parts)
                p_loc.append(jnp.concatenate(
                    [p_parts[rows.index(wr)].astype(BF16) if wr in rows else zero
                     for wr in range(win_rows)], axis=0))
                p_ctx.append(jnp.concatenate([p.astype(BF16) for p in p_parts[len(rows):]], axis=0))
                inv_l.append(1.0 / l)
        o_t = (_dot(vt_ref[0, :, pl.ds(k0, nk)], jnp.concatenate(p_loc, axis=1))
               + _dot(vct, jnp.concatenate(p_ctx, axis=1)))
        o_t = o_t * jnp.concatenate(inv_l, axis=1)
        o_ref[0, pl.ds(q0, Q_TILE), :] = _merge_heads_t(o_t, first_rows).astype(o_ref.dtype)

    tiles = [(0, 0, NA_WIN_ROWS, tab_all_ref, NA_WIN_ROWS - 1, True)]
    tiles += [(t * Q_TILE, t * Q_TILE - (NA_WIN_ROWS // 2) * GRID_W, NA_WIN_ROWS + Q_TILE_ROWS,
               tab_int_ref, NA_WIN_ROWS - 1 - NA_WIN_ROWS // 2, False) for t in range(1, n_tiles - 1)]
    tiles += [(seq - Q_TILE, seq - NA_WIN_ROWS * GRID_W, NA_WIN_ROWS, tab_all_ref, Q_TILE_ROWS - 1, True)]
    s_next = scores(*tiles[0])
    for n, cfg in enumerate(tiles):
        s_cur = s_next
        if n + 1 < len(tiles):
            s_next = scores(*tiles[n + 1])
        finish(*s_cur, *cfg)


def _na_attention(q, k, vt, qc, kc, vct, tab_int, tab_all):
    b, seq, _ = q.shape
    ctx_len = kc.shape[1]
    lat = pl.BlockSpec((1, seq, PAIR_W), lambda i, j: (i, 0, j))
    lat_t = pl.BlockSpec((1, PAIR_W, seq), lambda i, j: (i, j, 0))
    ctx = pl.BlockSpec((1, ctx_len, PAIR_W), lambda i, j: (i, 0, j))
    ctx_t = pl.BlockSpec((1, PAIR_W, ctx_len), lambda i, j: (i, j, 0))
    tab = pl.BlockSpec((HEADS_PER_BLOCK, N_DR0, GRID_W, PAIR_W), lambda i, j: (j, 0, 0, 0))
    with_ctx_out = qc is not None
    lat_shape = jax.ShapeDtypeStruct((b, seq, D_MODEL), BF16)
    ctx_shape = jax.ShapeDtypeStruct((b, ctx_len, D_MODEL), BF16)
    return pl.pallas_call(
        functools.partial(_na_kernel, with_ctx_out),
        grid=(b, N_PAIRS),
        in_specs=[lat, lat, lat_t] + ([ctx] if with_ctx_out else []) + [ctx, ctx_t, tab, tab],
        out_specs=[lat, ctx] if with_ctx_out else lat,
        out_shape=[lat_shape, ctx_shape] if with_ctx_out else lat_shape,
        compiler_params=_params(2),
        name="na_attention",
    )(q, k, vt, *([qc] if with_ctx_out else []), kc, vct, tab_int, tab_all)


def _bias_tables(rpb):
    n_dc = 2 * NA_WIN_COLS - 1
    kcol = np.arange(GRID_W)[:, None]
    qcol = np.arange(GRID_W)[None, :]
    cstart = np.clip(qcol - NA_WIN_COLS // 2, 0, GRID_W - NA_WIN_COLS)
    col_ok = (kcol >= cstart) & (kcol < cstart + NA_WIN_COLS)
    dr = np.arange(1, N_DR0 + 1)[:, None, None] - np.arange(2)[None, None, :]
    row_ok = (dr >= NA_WIN_ROWS - 1 - NA_WIN_ROWS // 2) & (dr < 2 * NA_WIN_ROWS - 1 - NA_WIN_ROWS // 2)
    ok_all = np.broadcast_to(col_ok[None, :, None, :], (N_DR0, GRID_W, 2, GRID_W))
    ok_int = ok_all & row_ok[..., None]
    rev = rpb[..., ::-1]
    ring = jnp.concatenate([rev[..., NA_WIN_COLS - 1:],
                            jnp.zeros(rpb.shape[:2] + (2 * GRID_W - n_dc,), rpb.dtype),
                            rev[..., :NA_WIN_COLS - 1]], axis=-1)
    skew = jnp.tile(ring, (1, 1, GRID_W))[..., :GRID_W * (2 * GRID_W - 1)]
    toep = skew.reshape(rpb.shape[:2] + (GRID_W, 2 * GRID_W - 1))[..., :GRID_W]
    vals = jnp.concatenate([toep[:, 1:N_DR0 + 1], toep[:, :N_DR0]], axis=-1).astype(F32)

    def table(ok):
        return jnp.where(ok.reshape(1, N_DR0, GRID_W, PAIR_W), vals, NEG_INF)

    return table(ok_int), table(ok_all)


def _gelu_tanh(x):
    c = float(np.sqrt(2.0 / np.pi))
    return x * (0.5 + 0.5 * jnp.tanh(x * (c + (c * 0.044715) * (x * x))))


def _gmlp_kernel(x_ref, mod_ref, n1g_ref, win_ref, bin_ref, lng_ref, lnb_ref, ws_ref, bs_ref, a_ref):
    mod = mod_ref[0]
    h = _norm_mod(x_ref[...], n1g_ref[...], mod[1:2], mod[0:1]).astype(BF16)
    z = _gelu_tanh(_dot(h, win_ref[...]) + bin_ref[...])
    u = z[:, :D_MODEL]
    v = z[:, D_MODEL:]
    mu = jnp.mean(v, axis=-1, keepdims=True)
    vc = v - mu
    var = jnp.mean(vc * vc, axis=-1, keepdims=True)
    vn = (vc * lax.rsqrt(var + EPS) * lng_ref[...] + lnb_ref[...]).astype(BF16)
    gw = D_MODEL // GMLP_GROUPS
    for n in range(x_ref.shape[0] // GMLP_CHUNK):
        rows = slice(n * GMLP_CHUNK, (n + 1) * GMLP_CHUNK)
        for g in range(GMLP_GROUPS):
            cols = slice(g * gw, (g + 1) * gw)
            mixed = _dot(ws_ref[g], vn[rows, cols]) + bs_ref[g]
            a_ref[rows, cols] = (u[rows, cols] * mixed).astype(a_ref.dtype)


def _gmlp(x, mod, rows_per_mod, n1g, w_in, b_in, ln_g, ln_b, w_s, b_s):
    t = x.shape[0]
    tok = lambda i: (i, 0)
    return pl.pallas_call(
        _gmlp_kernel,
        grid=(t // TM,),
        in_specs=[
            pl.BlockSpec((TM, D_MODEL), tok),
            pl.BlockSpec((1, 6, D_MODEL), lambda i: (i * TM // rows_per_mod, 0, 0)),
            _const_spec((1, D_MODEL)),
            _const_spec(w_in.shape),
            _const_spec(b_in.shape),
            _const_spec((1, D_MODEL)),
            _const_spec((1, D_MODEL)),
            _const_spec(w_s.shape),
            _const_spec(b_s.shape),
        ],
        out_specs=pl.BlockSpec((TM, D_MODEL), tok),
        out_shape=jax.ShapeDtypeStruct((t, D_MODEL), BF16),
        compiler_params=_params(),
        name="gmlp",
    )(x, mod, n1g, w_in, b_in, ln_g, ln_b, w_s, b_s)


def _pool_kernel(seq_len, x_ref, xp_ref, xn_ref, mod_ref, n1g_ref, a_ref, hbuf, sbuf):
    tm = x_ref.shape[0]
    tiles_per_seq = seq_len // tm
    pos = pl.program_id(0) % tiles_per_seq
    mod = mod_ref[0]
    n1g = n1g_ref[...]
    norm = lambda x: _norm_mod(x, n1g, mod[1:2], mod[0:1])
    zero = jnp.zeros((POOL_HALO, D_MODEL), F32)
    h0 = POOL_HALO
    hbuf[0:h0] = jnp.where(pos > 0, norm(xp_ref[...]), zero)
    hbuf[h0:h0 + tm] = norm(x_ref[...])
    hbuf[h0 + tm:h0 + tm + POOL_HALO] = jnp.where(pos < tiles_per_seq - 1, norm(xn_ref[...]), zero)
    hbuf[h0 + tm + POOL_HALO:] = jnp.zeros((POOL_PAD, D_MODEL), F32)
    t = pos * tm + lax.broadcasted_iota(jnp.int32, (tm, 1), 0)
    for g, w in enumerate(POOL_WINDOWS):
        cols = slice(g * POOL_GROUP, (g + 1) * POOL_GROUP)
        n_levels = g
        src = lambda lo, hi: hbuf[lo:hi, cols]
        for k in range(n_levels):
            step = 2 ** k
            rows = tm + POOL_HALO * (n_levels - k)
            sbuf[k, 0:rows] = src(0, rows) + src(step, rows + step)
            src = functools.partial(lambda k, lo, hi: sbuf[k, lo:hi], k)
        span = 2 ** n_levels
        lo = h0 - w // 2
        total = functools.reduce(jnp.add, [src(lo + j, lo + j + tm) for j in range(0, w, span)])
        cnt = jnp.minimum(t - w // 2 + w, seq_len) - jnp.maximum(t - w // 2, 0)
        mean = total * (1.0 / cnt.astype(F32))
        a_ref[:, cols] = (mean - hbuf[h0:h0 + tm, cols]).astype(a_ref.dtype)


def _pool(x, seq_len, tm, mod, rows_per_mod, n1g):
    t = x.shape[0]
    hb = tm // POOL_HALO
    n_halo = t // POOL_HALO
    n_levels = len(POOL_WINDOWS) - 1
    return pl.pallas_call(
        functools.partial(_pool_kernel, seq_len),
        grid=(t // tm,),
        in_specs=[
            pl.BlockSpec((tm, D_MODEL), lambda i: (i, 0)),
            pl.BlockSpec((POOL_HALO, D_MODEL), lambda i: (jnp.maximum(i * hb - 1, 0), 0)),
            pl.BlockSpec((POOL_HALO, D_MODEL), lambda i: (jnp.minimum((i + 1) * hb, n_halo - 1), 0)),
            pl.BlockSpec((1, 6, D_MODEL), lambda i: (i * tm // rows_per_mod, 0, 0)),
            _const_spec((1, D_MODEL)),
        ],
        out_specs=pl.BlockSpec((tm, D_MODEL), lambda i: (i, 0)),
        out_shape=jax.ShapeDtypeStruct((t, D_MODEL), BF16),
        scratch_shapes=[
            pltpu.VMEM((tm + 2 * POOL_HALO + POOL_PAD, D_MODEL), F32),
            pltpu.VMEM((n_levels, tm + POOL_HALO * n_levels, POOL_GROUP), F32),
        ],
        compiler_params=_params(),
        name="pool",
    )(x, x, x, mod, n1g)


def kernel(x, c, ctx, c_ctx, ada_w, ada_b, norm1_g, norm2_g, ffn_w_gate, ffn_w_up, ffn_w_down, na_w_qkv, na_w_o, na_q_norm, na_k_norm, na_rpb, gm_w_in, gm_b_in, gm_ln_g, gm_ln_b, gm_w_s, gm_b_s, gm_w_out, pool_w, pool_scale):
    b, seq, d = x.shape
    ctx_len = ctx.shape[1]
    assert d == D_MODEL and seq % TM == 0 and (b * ctx_len) % TM == 0 and seq % (Q_TILE * 2) == 0

    c_all = jnp.zeros((MOD_ROWS, d), F32).at[:b].set(c).at[b].set(c_ctx)
    mods = _ada_mods(c_all, ada_w, ada_b)

    heads_per_seg = 4
    seg = jnp.asarray(np.kron(np.eye(heads_per_seg), np.ones((NA_HEAD_DIM, NA_HEAD_DIM))), BF16)
    row = lambda v: v.reshape(1, d)
    tile_gain = lambda g: jnp.tile(g, NA_HEADS).reshape(1, d)

    xl = x.reshape(b * seq, d)
    xc = ctx.reshape(b * ctx_len, d)
    for i in range(DEPTH):
        kind, j = i % N_MIXERS, i // N_MIXERS
        last = i == DEPTH - 1
        need_ctx = (not last) or kind == 0
        mod_l = mods[i, :b].reshape(b, 6, d)
        mod_c = mods[i, b:b + 1].reshape(1, 6, d)
        streams = [(xl, mod_l, seq, seq)]
        if need_ctx:
            streams.append((xc, mod_c, b * ctx_len, ctx_len))
        n1g, n2g = row(norm1_g[i]), row(norm2_g[i])
        wg, wu, wd = (w[i].astype(BF16) for w in (ffn_w_gate, ffn_w_up, ffn_w_down))

        pool_s = None
        if kind == 0:
            w_qkv = na_w_qkv[j].astype(BF16)
            qg, kg = tile_gain(na_q_norm[j]), tile_gain(na_k_norm[j])
            (ql, kl, vtl), (qc, kc, vtc) = (
                _qkv(xs, sl, min(sl, TM), m, rpm, n1g, w_qkv, qg, kg, seg) for xs, m, rpm, sl in streams)
            ql, kl, qc, kc = (t.reshape(b, -1, d) for t in (ql, kl, qc, kc))
            tab_int, tab_all = _bias_tables(na_rpb[j])
            acts = _na_attention(ql, kl, vtl, None if last else qc, kc, vtc, tab_int, tab_all)
            acts = [a.reshape(-1, d) for a in (acts if isinstance(acts, (list, tuple)) else [acts])]
            mix_w = na_w_o[j].astype(BF16)
        elif kind == 1:
            w_in = gm_w_in[j].astype(BF16)
            b_in = gm_b_in[j].reshape(1, -1)
            w_s = gm_w_s[j].astype(BF16)
            b_s = jnp.broadcast_to(gm_b_s[j][:, :, None], (GMLP_GROUPS, GMLP_CHUNK, d // GMLP_GROUPS))
            acts = [_gmlp(xs, m, rpm, n1g, w_in, b_in, row(gm_ln_g[j]), row(gm_ln_b[j]), w_s, b_s)
                    for xs, m, rpm, _ in streams]
            mix_w = gm_w_out[j].astype(BF16)
        else:
            acts = [_pool(xs, sl, min(sl, POOL_TM), m, rpm, n1g) for xs, m, rpm, sl in streams]
            mix_w = pool_w[j].astype(BF16)
            pool_s = row(pool_scale[j])

        outs = [_post(a, xs, m, rpm, n2g, mix_w, pool_s, wg, wu, wd)
                for a, (xs, m, rpm, _) in zip(acts, streams)]
        xl = outs[0]
        if not last:
            xc = outs[1]
    return xl.reshape(b, seq, d)
```

```python
import functools

import numpy as np
import jax
import jax.numpy as jnp
from jax import lax
from jax.experimental import pallas as pl
from jax.experimental.pallas import tpu as pltpu

D_MODEL = 1024
DEPTH = 4
GRID_W = 64
N_MIXERS = 3
EPS = 1e-6
NEG_INF = -1e30

NA_HEADS = 16
NA_HEAD_DIM = D_MODEL // NA_HEADS
NA_WIN_ROWS = 8
NA_WIN_COLS = 16
HEADS_PER_BLOCK = 2
PAIR_W = HEADS_PER_BLOCK * NA_HEAD_DIM
N_PAIRS = NA_HEADS // HEADS_PER_BLOCK
Q_TILE_ROWS = 4
Q_TILE = Q_TILE_ROWS * GRID_W
N_DR0 = 2 * NA_WIN_ROWS - 2

GMLP_CHUNK = 128
GMLP_GROUPS = 8
POOL_WINDOWS = (2, 4, 8, 16)
POOL_GROUP = D_MODEL // len(POOL_WINDOWS)
POOL_HALO = 8
POOL_PAD = 16
POOL_CHUNK = 32

MOD_ROWS = 32
ADA_TN = 1024
TM = 512
POST_ROW_BLOCKS = 2
POOL_TM = 512
VMEM_LIMIT_BYTES = 56 * 1024 * 1024

BF16 = jnp.bfloat16
F32 = jnp.float32


def _dot(a, b):
    return jnp.dot(a, b, preferred_element_type=F32)


def _dot_nt(a, b):
    return lax.dot_general(a, b, (((1,), (1,)), ((), ())), preferred_element_type=F32)


def _silu(x):
    return x / (1.0 + jnp.exp(-x))


def _const_spec(shape):
    nd = len(shape)
    return pl.BlockSpec(shape, lambda *_: (0,) * nd, pipeline_mode=pl.Buffered(1))


def _params(n_axes=1):
    return pltpu.CompilerParams(
        dimension_semantics=("arbitrary",) * n_axes, vmem_limit_bytes=VMEM_LIMIT_BYTES)


def _norm_mod(x, g, sc, sh):
    ms = jnp.mean(x * x, axis=-1, keepdims=True)
    return (x * lax.rsqrt(ms + EPS)) * (g * (1.0 + sc)) + sh


def _ada_kernel(c_ref, w_ref, b_ref, o_ref):
    s = _silu(c_ref[...]).astype(BF16)
    o_ref[0] = _dot(s, w_ref[0].astype(BF16)) + b_ref[0]


def _ada_mods(c_all, ada_w, ada_b):
    n = ada_w.shape[-1]
    return pl.pallas_call(
        _ada_kernel,
        grid=(DEPTH, n // ADA_TN),
        in_specs=[
            pl.BlockSpec((MOD_ROWS, D_MODEL), lambda i, j: (0, 0)),
            pl.BlockSpec((1, D_MODEL, ADA_TN), lambda i, j: (i, 0, j)),
            pl.BlockSpec((1, 1, ADA_TN), lambda i, j: (i, 0, j)),
        ],
        out_specs=pl.BlockSpec((1, MOD_ROWS, ADA_TN), lambda i, j: (i, 0, j)),
        out_shape=jax.ShapeDtypeStruct((DEPTH, MOD_ROWS, n), F32),
        compiler_params=_params(2),
        name="ada_mods",
    )(c_all, ada_w, ada_b.reshape(DEPTH, 1, n))


def _post_kernel(pooled, a_ref, x_ref, mod_ref, n2g_ref, *rest):
    if pooled:
        wp_ref, ps_ref, wg_ref, wu_ref, wd_ref, o_ref = rest
    else:
        wm_ref, wg_ref, wu_ref, wd_ref, o_ref = rest
    mod = mod_ref[0]
    g1, sh2, sc2, g2 = mod[2:3], mod[3:4], mod[4:5], mod[5:6]
    rows_per_block = x_ref.shape[0] // POST_ROW_BLOCKS
    blocks = [pl.ds(r * rows_per_block, rows_per_block) for r in range(POST_ROW_BLOCKS)]

    def mix(rows):
        a = a_ref[rows, :]
        if pooled:
            return jnp.concatenate(
                [_dot(a[:, g * POOL_GROUP:(g + 1) * POOL_GROUP], wp_ref[g])
                 for g in range(len(POOL_WINDOWS))], axis=-1) * ps_ref[...]
        return _dot(a, wm_ref[...])

    ys = [mix(rows) for rows in blocks]
    x1s = [x_ref[rows, :] + g1 * y for rows, y in zip(blocks, ys)]
    h2s = [_norm_mod(x1, n2g_ref[...], sc2, sh2).astype(BF16) for x1 in x1s]
    gus = [(_dot(h2, wg_ref[...]), _dot(h2, wu_ref[...])) for h2 in h2s]
    acts = [(_silu(gate) * up).astype(BF16) for gate, up in gus]
    for rows, x1, act in zip(blocks, x1s, acts):
        o_ref[rows, :] = x1 + g2 * _dot(act, wd_ref[...])


def _post(a, x, mod, rows_per_mod, n2g, mix_w, pool_scale, wg, wu, wd):
    t = x.shape[0]
    pooled = pool_scale is not None
    tok = lambda i: (i, 0)
    in_specs = [
        pl.BlockSpec((TM, D_MODEL), tok),
        pl.BlockSpec((TM, D_MODEL), tok),
        pl.BlockSpec((1, 6, D_MODEL), lambda i: (i * TM // rows_per_mod, 0, 0)),
        _const_spec((1, D_MODEL)),
        _const_spec(mix_w.shape),
    ]
    args = [a, x, mod, n2g, mix_w]
    if pooled:
        in_specs.append(_const_spec((1, D_MODEL)))
        args.append(pool_scale)
    in_specs += [_const_spec(wg.shape), _const_spec(wu.shape), _const_spec(wd.shape)]
    args += [wg, wu, wd]
    return pl.pallas_call(
        functools.partial(_post_kernel, pooled),
        grid=(t // TM,),
        in_specs=in_specs,
        out_specs=pl.BlockSpec((TM, D_MODEL), tok),
        out_shape=jax.ShapeDtypeStruct((t, D_MODEL), F32),
        compiler_params=_params(),
        name="post_pool" if pooled else "post_dense",
    )(*args)


def _qkv_kernel(x_ref, mod_ref, n1g_ref, w_ref, qg_ref, kg_ref, seg_ref, q_ref, k_ref, v_ref):
    mod = mod_ref[0]
    seg = seg_ref[...]
    seg_w = seg.shape[0]

    def mean_sq(t):
        return jnp.concatenate(
            [_dot((t[:, c:c + seg_w] * t[:, c:c + seg_w]).astype(BF16), seg)
             for c in range(0, D_MODEL, seg_w)], axis=-1) * (1.0 / NA_HEAD_DIM)

    rows_per_block = x_ref.shape[0] // POST_ROW_BLOCKS
    blocks = [pl.ds(r * rows_per_block, rows_per_block) for r in range(POST_ROW_BLOCKS)]
    hs = [_norm_mod(x_ref[rows, :], n1g_ref[...], mod[1:2], mod[0:1]).astype(BF16) for rows in blocks]
    qkvs = [_dot(h, w_ref[...]) for h in hs]
    for rows, qkv in zip(blocks, qkvs):
        v_ref[rows, :] = qkv[:, 2 * D_MODEL:].astype(BF16)
    qs = [qkv[:, :D_MODEL] for qkv in qkvs]
    ks = [qkv[:, D_MODEL:2 * D_MODEL] for qkv in qkvs]
    q_ms = [mean_sq(q) for q in qs]
    k_ms = [mean_sq(k) for k in ks]
    for rows, q, k, qm, km in zip(blocks, qs, ks, q_ms, k_ms):
        q_ref[rows, :] = (q * lax.rsqrt(qm + EPS) * qg_ref[...] * (NA_HEAD_DIM ** -0.5)).astype(BF16)
        k_ref[rows, :] = (k * lax.rsqrt(km + EPS) * kg_ref[...]).astype(BF16)


def _qkv(x, mod, rows_per_mod, n1g, w_qkv, qg, kg, seg):
    t = x.shape[0]
    tok = lambda i: (i, 0)
    out = jax.ShapeDtypeStruct((t, D_MODEL), BF16)
    return pl.pallas_call(
        _qkv_kernel,
        grid=(t // TM,),
        in_specs=[
            pl.BlockSpec((TM, D_MODEL), tok),
            pl.BlockSpec((1, 6, D_MODEL), lambda i: (i * TM // rows_per_mod, 0, 0)),
            _const_spec((1, D_MODEL)),
            _const_spec(w_qkv.shape),
            _const_spec((1, D_MODEL)),
            _const_spec((1, D_MODEL)),
            _const_spec(seg.shape),
        ],
        out_specs=[pl.BlockSpec((TM, D_MODEL), tok)] * 3,
        out_shape=[out, out, out],
        compiler_params=_params(),
        name="qkv",
    )(x, mod, n1g, w_qkv, qg, kg, seg)


def _softmax_rows(s_parts, biases):
    biased = [s if b is None else s + b for s, b in zip(s_parts, biases)]
    m = jnp.max(functools.reduce(jnp.maximum, biased), axis=-1, keepdims=True)
    p_parts = [jnp.exp(s - m) for s in biased]
    l = jnp.sum(functools.reduce(jnp.add, p_parts), axis=-1, keepdims=True)
    return p_parts, l


def _stack_heads(qt, first):
    zero = jnp.zeros_like(qt)
    return jnp.concatenate([jnp.where(first, qt, zero), jnp.where(first, zero, qt)], axis=0)


def _na_kernel(with_ctx_out, q_ref, k_ref, v_ref, *rest):
    if with_ctx_out:
        qc_ref, kc_ref, vc_ref, tab_int_ref, tab_all_ref, o_ref, oc_ref = rest
    else:
        kc_ref, vc_ref, tab_int_ref, tab_all_ref, o_ref = rest
    seq = q_ref.shape[1]
    n_tiles = seq // Q_TILE
    first = lax.broadcasted_iota(jnp.int32, (1, PAIR_W), 1) < NA_HEAD_DIM
    kc = kc_ref[0]
    vc = vc_ref[0]
    ctx_parts = range(0, kc.shape[0], PAIR_W)

    if with_ctx_out:
        n = qc_ref.shape[1]
        (p,), l = _softmax_rows([_dot_nt(_stack_heads(qc_ref[0], first), kc)], [None])
        oc = _dot(p.astype(BF16), vc) * (1.0 / l)
        oc_ref[0] = jnp.where(first, oc[:n], oc[n:]).astype(oc_ref.dtype)

    def scores(q0, k0, win_rows, *_):
        qs = _stack_heads(q_ref[0, pl.ds(q0, Q_TILE), :], first)
        return _dot_nt(qs, k_ref[0, pl.ds(k0, win_rows * GRID_W), :]), _dot_nt(qs, kc)

    def finish(s_loc, s_ctx, q0, k0, win_rows, tab_ref, dr_base, clamped):
        p_loc, p_ctx, inv_l = [], [], []
        zero = jnp.zeros((GRID_W, PAIR_W), BF16)
        for hh in range(HEADS_PER_BLOCK):
            for i in range(Q_TILE_ROWS):
                r0 = hh * Q_TILE + i * GRID_W
                lo, hi = (0, win_rows) if clamped else (i, i + NA_WIN_ROWS)
                pairs = range(lo // 2, (hi + 1) // 2)
                parts = [s_loc[r0:r0 + GRID_W, wp * PAIR_W:(wp + 1) * PAIR_W] for wp in pairs]
                biases = [tab_ref[hh, 2 * wp - i + dr_base] for wp in pairs]
                parts += [s_ctx[r0:r0 + GRID_W, c:c + PAIR_W] for c in ctx_parts]
                biases += [None] * len(ctx_parts)
                p_parts, l = _softmax_rows(parts, biases)
                p_loc.append(jnp.concatenate(
                    [p_parts[pairs.index(wp)].astype(BF16) if wp in pairs else zero
                     for wp in range(win_rows // 2)], axis=-1))
                p_ctx.append(jnp.concatenate([p.astype(BF16) for p in p_parts[len(pairs):]], axis=-1))
                inv_l.append(1.0 / l)
        vw = v_ref[0, pl.ds(k0, win_rows * GRID_W), :]
        o = _dot(jnp.concatenate(p_loc, axis=0), vw) + _dot(jnp.concatenate(p_ctx, axis=0), vc)
        o = o * jnp.concatenate(inv_l, axis=0)
        o_ref[0, pl.ds(q0, Q_TILE), :] = jnp.where(first, o[:Q_TILE], o[Q_TILE:]).astype(o_ref.dtype)

    tiles = [(0, 0, NA_WIN_ROWS, tab_all_ref, NA_WIN_ROWS - 1, True),
             (seq - Q_TILE, seq - NA_WIN_ROWS * GRID_W, NA_WIN_ROWS, tab_all_ref, Q_TILE_ROWS - 1, True)]
    tiles += [(t * Q_TILE, t * Q_TILE - (NA_WIN_ROWS // 2) * GRID_W, NA_WIN_ROWS + Q_TILE_ROWS,
               tab_int_ref, NA_WIN_ROWS - 1 - NA_WIN_ROWS // 2, False) for t in range(1, n_tiles - 1)]
    for cfg in tiles:
        finish(*scores(*cfg), *cfg)


def _na_attention(q, k, v, qc, kc, vc, tab_int, tab_all):
    b, seq, _ = q.shape
    ctx_len = kc.shape[1]
    lat = pl.BlockSpec((1, seq, PAIR_W), lambda i, j: (i, 0, j))
    ctx = pl.BlockSpec((1, ctx_len, PAIR_W), lambda i, j: (i, 0, j))
    tab = pl.BlockSpec((HEADS_PER_BLOCK, N_DR0, GRID_W, PAIR_W), lambda i, j: (j, 0, 0, 0))
    with_ctx_out = qc is not None
    lat_shape = jax.ShapeDtypeStruct((b, seq, D_MODEL), BF16)
    ctx_shape = jax.ShapeDtypeStruct((b, ctx_len, D_MODEL), BF16)
    return pl.pallas_call(
        functools.partial(_na_kernel, with_ctx_out),
        grid=(b, N_PAIRS),
        in_specs=[lat, lat, lat] + [ctx] * (3 if with_ctx_out else 2) + [tab, tab],
        out_specs=[lat, ctx] if with_ctx_out else lat,
        out_shape=[lat_shape, ctx_shape] if with_ctx_out else lat_shape,
        compiler_params=_params(2),
        name="na_attention",
    )(q, k, v, *([qc] if with_ctx_out else []), kc, vc, tab_int, tab_all)


def _bias_tables(rpb):
    n_dc = 2 * NA_WIN_COLS - 1
    qcol = np.arange(GRID_W)[:, None]
    kcol = np.arange(GRID_W)[None, :]
    cstart = np.clip(qcol - NA_WIN_COLS // 2, 0, GRID_W - NA_WIN_COLS)
    col_ok = (kcol >= cstart) & (kcol < cstart + NA_WIN_COLS)
    dr = np.arange(N_DR0)[:, None, None] + np.arange(2)[None, None, :]
    row_ok = (dr >= NA_WIN_ROWS - 1 - NA_WIN_ROWS // 2) & (dr < 2 * NA_WIN_ROWS - 1 - NA_WIN_ROWS // 2)
    ok_all = np.broadcast_to(col_ok[None, :, None, :], (N_DR0, GRID_W, 2, GRID_W))
    ok_int = ok_all & row_ok[..., None]
    dc = kcol - qcol + NA_WIN_COLS - 1
    toep = functools.reduce(jnp.add, [
        jnp.where(dc == j, rpb[:, :, j, None, None].astype(F32), 0.0) for j in range(n_dc)])
    vals = jnp.concatenate([toep[:, :N_DR0], toep[:, 1:N_DR0 + 1]], axis=-1)

    def table(ok):
        return jnp.where(ok.reshape(1, N_DR0, GRID_W, PAIR_W), vals, NEG_INF)

    return table(ok_int), table(ok_all)


def _gelu_tanh(x):
    c = float(np.sqrt(2.0 / np.pi))
    return x * (0.5 + 0.5 * jnp.tanh(x * (c + (c * 0.044715) * (x * x))))


def _gmlp_kernel(x_ref, mod_ref, n1g_ref, win_ref, bin_ref, lng_ref, lnb_ref, ws_ref, bs_ref, a_ref):
    mod = mod_ref[0]
    gw = D_MODEL // GMLP_GROUPS

    def layer_norm(v):
        mu = jnp.mean(v, axis=-1, keepdims=True)
        vc = v - mu
        var = jnp.mean(vc * vc, axis=-1, keepdims=True)
        return (vc * lax.rsqrt(var + EPS) * lng_ref[...] + lnb_ref[...]).astype(BF16)

    rows_per_block = x_ref.shape[0] // POST_ROW_BLOCKS
    blocks = [r * rows_per_block for r in range(POST_ROW_BLOCKS)]
    hs = [_norm_mod(x_ref[pl.ds(r0, rows_per_block), :], n1g_ref[...], mod[1:2], mod[0:1]).astype(BF16)
          for r0 in blocks]
    zs = [_dot(h, win_ref[...]) for h in hs]
    zs = [_gelu_tanh(z + bin_ref[...]) for z in zs]
    vns = [layer_norm(z[:, D_MODEL:]) for z in zs]
    for r0, z, vn in zip(blocks, zs, vns):
        for n in range(rows_per_block // GMLP_CHUNK):
            rows = slice(n * GMLP_CHUNK, (n + 1) * GMLP_CHUNK)
            for g in range(GMLP_GROUPS):
                cols = slice(g * gw, (g + 1) * gw)
                mixed = _dot(ws_ref[g], vn[rows, cols]) + bs_ref[g]
                a_ref[pl.ds(r0 + n * GMLP_CHUNK, GMLP_CHUNK), cols] = (
                    z[rows, cols] * mixed).astype(a_ref.dtype)


def _gmlp(x, mod, rows_per_mod, n1g, w_in, b_in, ln_g, ln_b, w_s, b_s):
    t = x.shape[0]
    tok = lambda i: (i, 0)
    return pl.pallas_call(
        _gmlp_kernel,
        grid=(t // TM,),
        in_specs=[
            pl.BlockSpec((TM, D_MODEL), tok),
            pl.BlockSpec((1, 6, D_MODEL), lambda i: (i * TM // rows_per_mod, 0, 0)),
            _const_spec((1, D_MODEL)),
            _const_spec(w_in.shape),
            _const_spec(b_in.shape),
            _const_spec((1, D_MODEL)),
            _const_spec((1, D_MODEL)),
            _const_spec(w_s.shape),
            _const_spec(b_s.shape),
        ],
        out_specs=pl.BlockSpec((TM, D_MODEL), tok),
        out_shape=jax.ShapeDtypeStruct((t, D_MODEL), BF16),
        compiler_params=_params(),
        name="gmlp",
    )(x, mod, n1g, w_in, b_in, ln_g, ln_b, w_s, b_s)


def _pool_kernel(seq_len, x_ref, xp_ref, xn_ref, mod_ref, n1g_ref, a_ref, hbuf, sbuf):
    tm = x_ref.shape[0]
    tiles_per_seq = seq_len // tm
    pos = pl.program_id(0) % tiles_per_seq
    mod = mod_ref[0]
    n1g = n1g_ref[...]
    norm = lambda x: _norm_mod(x, n1g, mod[1:2], mod[0:1])
    zero = jnp.zeros((POOL_HALO, D_MODEL), F32)
    h0 = POOL_HALO
    chunks = lambda n: [(r, min(POOL_CHUNK, n - r)) for r in range(0, n, POOL_CHUNK)]
    hbuf[0:h0] = jnp.where(pos > 0, norm(xp_ref[...]), zero)
    for r, n in chunks(tm):
        hbuf[h0 + r:h0 + r + n] = norm(x_ref[r:r + n])
    hbuf[h0 + tm:h0 + tm + POOL_HALO] = jnp.where(pos < tiles_per_seq - 1, norm(xn_ref[...]), zero)
    hbuf[h0 + tm + POOL_HALO:] = jnp.zeros((POOL_PAD, D_MODEL), F32)
    for g, w in enumerate(POOL_WINDOWS):
        cols = slice(g * POOL_GROUP, (g + 1) * POOL_GROUP)
        n_levels = g
        src = lambda lo, hi: hbuf[lo:hi, cols]
        for k in range(n_levels):
            step = 2 ** k
            for r, n in chunks(tm + POOL_HALO * (n_levels - k)):
                sbuf[k, r:r + n] = src(r, r + n) + src(r + step, r + n + step)
            src = functools.partial(lambda k, lo, hi: sbuf[k, lo:hi], k)
        span = 2 ** n_levels
        lo = h0 - w // 2
        for r, n in chunks(tm):
            total = functools.reduce(jnp.add, [src(lo + r + j, lo + r + j + n) for j in range(0, w, span)])
            t = pos * tm + r + lax.broadcasted_iota(jnp.int32, (n, 1), 0)
            cnt = jnp.minimum(t - w // 2 + w, seq_len) - jnp.maximum(t - w // 2, 0)
            mean = total * (1.0 / cnt.astype(F32))
            a_ref[r:r + n, cols] = (mean - hbuf[h0 + r:h0 + r + n, cols]).astype(a_ref.dtype)


def _pool(x, seq_len, tm, mod, rows_per_mod, n1g):
    t = x.shape[0]
    hb = tm // POOL_HALO
    n_halo = t // POOL_HALO
    n_levels = len(POOL_WINDOWS) - 1
    return pl.pallas_call(
        functools.partial(_pool_kernel, seq_len),
        grid=(t // tm,),
        in_specs=[
            pl.BlockSpec((tm, D_MODEL), lambda i: (i, 0)),
            pl.BlockSpec((POOL_HALO, D_MODEL), lambda i: (jnp.maximum(i * hb - 1, 0), 0)),
            pl.BlockSpec((POOL_HALO, D_MODEL), lambda i: (jnp.minimum((i + 1) * hb, n_halo - 1), 0)),
            pl.BlockSpec((1, 6, D_MODEL), lambda i: (i * tm // rows_per_mod, 0, 0)),
            _const_spec((1, D_MODEL)),
        ],
        out_specs=pl.BlockSpec((tm, D_MODEL), lambda i: (i, 0)),
        out_shape=jax.ShapeDtypeStruct((t, D_MODEL), BF16),
        scratch_shapes=[
            pltpu.VMEM((tm + 2 * POOL_HALO + POOL_PAD, D_MODEL), F32),
            pltpu.VMEM((n_levels, tm + POOL_HALO * n_levels, POOL_GROUP), F32),
        ],
        compiler_params=_params(),
        name="pool",
    )(x, x, x, mod, n1g)


def kernel(x, c, ctx, c_ctx, ada_w, ada_b, norm1_g, norm2_g, ffn_w_gate, ffn_w_up, ffn_w_down, na_w_qkv, na_w_o, na_q_norm, na_k_norm, na_rpb, gm_w_in, gm_b_in, gm_ln_g, gm_ln_b, gm_w_s, gm_b_s, gm_w_out, pool_w, pool_scale):
    b, seq, d = x.shape
    ctx_len = ctx.shape[1]
    assert d == D_MODEL and seq % TM == 0 and (b * ctx_len) % TM == 0 and seq % (Q_TILE * 2) == 0

    c_all = jnp.zeros((MOD_ROWS, d), F32).at[:b].set(c).at[b].set(c_ctx)
    mods = _ada_mods(c_all, ada_w, ada_b)

    heads_per_seg = 4
    seg = jnp.asarray(np.kron(np.eye(heads_per_seg), np.ones((NA_HEAD_DIM, NA_HEAD_DIM))), BF16)
    row = lambda v: v.reshape(1, d)
    tile_gain = lambda g: jnp.tile(g, NA_HEADS).reshape(1, d)

    xl = x.reshape(b * seq, d)
    xc = ctx.reshape(b * ctx_len, d)
    for i in range(DEPTH):
        kind, j = i % N_MIXERS, i // N_MIXERS
        last = i == DEPTH - 1
        need_ctx = (not last) or kind == 0
        mod_l = mods[i, :b].reshape(b, 6, d)
        mod_c = mods[i, b:b + 1].reshape(1, 6, d)
        streams = [(xl, mod_l, seq, seq)]
        if need_ctx:
            streams.append((xc, mod_c, b * ctx_len, ctx_len))
        n1g, n2g = row(norm1_g[i]), row(norm2_g[i])
        wg, wu, wd = (w[i].astype(BF16) for w in (ffn_w_gate, ffn_w_up, ffn_w_down))

        pool_s = None
        if kind == 0:
            w_qkv = na_w_qkv[j].astype(BF16)
            qg, kg = tile_gain(na_q_norm[j]), tile_gain(na_k_norm[j])
            (ql, kl, vl), (qc, kc, vc) = (
                tuple(t.reshape(b, -1, d) for t in _qkv(xs, m, rpm, n1g, w_qkv, qg, kg, seg))
                for xs, m, rpm, _ in streams)
            tab_int, tab_all = _bias_tables(na_rpb[j])
            acts = _na_attention(ql, kl, vl, None if last else qc, kc, vc, tab_int, tab_all)
            acts = [a.reshape(-1, d) for a in (acts if isinstance(acts, (list, tuple)) else [acts])]
            mix_w = na_w_o[j].astype(BF16)
        elif kind == 1:
            w_in = gm_w_in[j].astype(BF16)
            b_in = gm_b_in[j].reshape(1, -1)
            w_s = gm_w_s[j].astype(BF16)
            b_s = jnp.broadcast_to(gm_b_s[j][:, :, None], (GMLP_GROUPS, GMLP_CHUNK, d // GMLP_GROUPS))
            acts = [_gmlp(xs, m, rpm, n1g, w_in, b_in, row(gm_ln_g[j]), row(gm_ln_b[j]), w_s, b_s)
                    for xs, m, rpm, _ in streams]
            mix_w = gm_w_out[j].astype(BF16)
        else:
            acts = [_pool(xs, sl, min(sl, POOL_TM), m, rpm, n1g) for xs, m, rpm, sl in streams]
            mix_w = pool_w[j].astype(BF16)
            pool_s = row(pool_scale[j])

        outs = [_post(a, xs, m, rpm, n2g, mix_w, pool_s, wg, wu, wd)
                for a, (xs, m, rpm, _) in zip(acts, streams)]
        xl = outs[0]
        if not last:
            xc = outs[1]
    return xl.reshape(b, seq, d)
```

```python
import functools

import numpy as np
import jax
import jax.numpy as jnp
from jax import lax
from jax.experimental import pallas as pl
from jax.experimental.pallas import tpu as pltpu

D_MODEL = 1024
DEPTH = 4
GRID_W = 64
N_MIXERS = 3
EPS = 1e-6
NEG_INF = -1e30

NA_HEADS = 16
NA_HEAD_DIM = D_MODEL // NA_HEADS
NA_WIN_ROWS = 8
NA_WIN_COLS = 16
HEADS_PER_BLOCK = 2
PAIR_W = HEADS_PER_BLOCK * NA_HEAD_DIM
N_PAIRS = NA_HEADS // HEADS_PER_BLOCK
Q_TILE_ROWS = 4
Q_TILE = Q_TILE_ROWS * GRID_W
N_DR0 = 2 * NA_WIN_ROWS - 2
DR_LO = NA_WIN_ROWS - 1 - NA_WIN_ROWS // 2
DR_HI = DR_LO + NA_WIN_ROWS - 1

GMLP_CHUNK = 128
GMLP_GROUPS = 8
POOL_WINDOWS = (2, 4, 8, 16)
POOL_GROUP = D_MODEL // len(POOL_WINDOWS)
POOL_HALO = 8
POOL_PAD = 16
POOL_CHUNK = 32

MOD_ROWS = 32
ADA_TN = 1024
TM = 512
POST_ROW_BLOCKS = 2
POOL_TM = 512
VMEM_LIMIT_BYTES = 56 * 1024 * 1024

BF16 = jnp.bfloat16
F32 = jnp.float32


def _dot(a, b):
    return jnp.dot(a, b, preferred_element_type=F32)


def _dot_nt(a, b):
    return lax.dot_general(a, b, (((1,), (1,)), ((), ())), preferred_element_type=F32)


def _silu(x):
    return x / (1.0 + jnp.exp(-x))


def _const_spec(shape):
    nd = len(shape)
    return pl.BlockSpec(shape, lambda *_: (0,) * nd, pipeline_mode=pl.Buffered(1))


def _layer_spec(shape, layer):
    nd = len(shape)
    return pl.BlockSpec((1,) + tuple(shape[1:]), lambda *_: (layer,) + (0,) * (nd - 1),
                        pipeline_mode=pl.Buffered(1))


def _params(n_axes=1):
    return pltpu.CompilerParams(
        dimension_semantics=("arbitrary",) * n_axes, vmem_limit_bytes=VMEM_LIMIT_BYTES)


def _norm_mod(x, g, sc, sh):
    ms = jnp.mean(x * x, axis=-1, keepdims=True)
    return (x * lax.rsqrt(ms + EPS)) * (g * (1.0 + sc)) + sh


def _ada_kernel(c_ref, w_ref, b_ref, o_ref):
    s = _silu(c_ref[...]).astype(BF16)
    o_ref[0] = _dot(s, w_ref[0].astype(BF16)) + b_ref[0]


def _ada_mods(c_all, ada_w, ada_b):
    n = ada_w.shape[-1]
    return pl.pallas_call(
        _ada_kernel,
        grid=(DEPTH, n // ADA_TN),
        in_specs=[
            pl.BlockSpec((MOD_ROWS, D_MODEL), lambda i, j: (0, 0)),
            pl.BlockSpec((1, D_MODEL, ADA_TN), lambda i, j: (i, 0, j)),
            pl.BlockSpec((1, 1, ADA_TN), lambda i, j: (i, 0, j)),
        ],
        out_specs=pl.BlockSpec((1, MOD_ROWS, ADA_TN), lambda i, j: (i, 0, j)),
        out_shape=jax.ShapeDtypeStruct((DEPTH, MOD_ROWS, n), F32),
        compiler_params=_params(2),
        name="ada_mods",
    )(c_all, ada_w, ada_b.reshape(DEPTH, 1, n))


def _post_kernel(pooled, a_ref, x_ref, mod_ref, n2g_ref, *rest):
    if pooled:
        wp_ref, ps_ref, wg_ref, wu_ref, wd_ref, o_ref = rest
    else:
        wm_ref, wg_ref, wu_ref, wd_ref, o_ref = rest
    mod = mod_ref[0]
    g1, sh2, sc2, g2 = mod[2:3], mod[3:4], mod[4:5], mod[5:6]
    rows_per_block = x_ref.shape[0] // POST_ROW_BLOCKS
    blocks = [pl.ds(r * rows_per_block, rows_per_block) for r in range(POST_ROW_BLOCKS)]

    def mix(rows):
        a = a_ref[rows, :]
        if pooled:
            return jnp.concatenate(
                [_dot(a[:, g * POOL_GROUP:(g + 1) * POOL_GROUP], wp_ref[g])
                 for g in range(len(POOL_WINDOWS))], axis=-1) * ps_ref[...]
        return _dot(a, wm_ref[...])

    ys = [mix(rows) for rows in blocks]
    x1s = [x_ref[rows, :] + g1 * y for rows, y in zip(blocks, ys)]
    h2s = [_norm_mod(x1, n2g_ref[...], sc2, sh2).astype(BF16) for x1 in x1s]
    gus = [(_dot(h2, wg_ref[0]), _dot(h2, wu_ref[0])) for h2 in h2s]
    acts = [(_silu(gate) * up).astype(BF16) for gate, up in gus]
    for rows, x1, act in zip(blocks, x1s, acts):
        o_ref[rows, :] = x1 + g2 * _dot(act, wd_ref[0])


def _post(a, x, mod, rows_per_mod, n2g, mix_w, pool_scale, layer, wg, wu, wd):
    t = x.shape[0]
    pooled = pool_scale is not None
    tok = lambda i: (i, 0)
    in_specs = [
        pl.BlockSpec((TM, D_MODEL), tok),
        pl.BlockSpec((TM, D_MODEL), tok),
        pl.BlockSpec((1, 6, D_MODEL), lambda i: (i * TM // rows_per_mod, 0, 0)),
        _const_spec((1, D_MODEL)),
        _const_spec(mix_w.shape),
    ]
    args = [a, x, mod, n2g, mix_w]
    if pooled:
        in_specs.append(_const_spec((1, D_MODEL)))
        args.append(pool_scale)
    in_specs += [_layer_spec(w.shape, layer) for w in (wg, wu, wd)]
    args += [wg, wu, wd]
    return pl.pallas_call(
        functools.partial(_post_kernel, pooled),
        grid=(t // TM,),
        in_specs=in_specs,
        out_specs=pl.BlockSpec((TM, D_MODEL), tok),
        out_shape=jax.ShapeDtypeStruct((t, D_MODEL), F32),
        compiler_params=_params(),
        name="post_pool" if pooled else "post_dense",
    )(*args)


def _qkv_kernel(x_ref, mod_ref, n1g_ref, w_ref, qg_ref, kg_ref, seg_ref, q_ref, k_ref, v_ref):
    mod = mod_ref[0]
    seg = seg_ref[...]
    seg_w = seg.shape[0]

    def mean_sq(t):
        return jnp.concatenate(
            [_dot((t[:, c:c + seg_w] * t[:, c:c + seg_w]).astype(BF16), seg)
             for c in range(0, D_MODEL, seg_w)], axis=-1) * (1.0 / NA_HEAD_DIM)

    rows_per_block = x_ref.shape[0] // POST_ROW_BLOCKS
    blocks = [pl.ds(r * rows_per_block, rows_per_block) for r in range(POST_ROW_BLOCKS)]
    hs = [_norm_mod(x_ref[rows, :], n1g_ref[...], mod[1:2], mod[0:1]).astype(BF16) for rows in blocks]
    qkvs = [_dot(h, w_ref[...]) for h in hs]
    for rows, qkv in zip(blocks, qkvs):
        v_ref[rows, :] = qkv[:, 2 * D_MODEL:].astype(BF16)
    qs = [qkv[:, :D_MODEL] for qkv in qkvs]
    ks = [qkv[:, D_MODEL:2 * D_MODEL] for qkv in qkvs]
    q_ms = [mean_sq(q) for q in qs]
    k_ms = [mean_sq(k) for k in ks]
    for rows, q, k, qm, km in zip(blocks, qs, ks, q_ms, k_ms):
        q_ref[rows, :] = (q * lax.rsqrt(qm + EPS) * qg_ref[...] * (NA_HEAD_DIM ** -0.5)).astype(BF16)
        k_ref[rows, :] = (k * lax.rsqrt(km + EPS) * kg_ref[...]).astype(BF16)


def _qkv(x, mod, rows_per_mod, n1g, w_qkv, qg, kg, seg):
    t = x.shape[0]
    tok = lambda i: (i, 0)
    out = jax.ShapeDtypeStruct((t, D_MODEL), BF16)
    return pl.pallas_call(
        _qkv_kernel,
        grid=(t // TM,),
        in_specs=[
            pl.BlockSpec((TM, D_MODEL), tok),
            pl.BlockSpec((1, 6, D_MODEL), lambda i: (i * TM // rows_per_mod, 0, 0)),
            _const_spec((1, D_MODEL)),
            _const_spec(w_qkv.shape),
            _const_spec((1, D_MODEL)),
            _const_spec((1, D_MODEL)),
            _const_spec(seg.shape),
        ],
        out_specs=[pl.BlockSpec((TM, D_MODEL), tok)] * 3,
        out_shape=[out, out, out],
        compiler_params=_params(),
        name="qkv",
    )(x, mod, n1g, w_qkv, qg, kg, seg)


def _softmax_rows(s_parts, biases):
    biased = [s if b is None else s + b for s, b in zip(s_parts, biases)]
    m = jnp.max(functools.reduce(jnp.maximum, biased), axis=-1, keepdims=True)
    p_parts = [jnp.exp(s - m) for s in biased]
    l = jnp.sum(functools.reduce(jnp.add, p_parts), axis=-1, keepdims=True)
    return p_parts, l


def _stack_heads(qt, first):
    zero = jnp.zeros_like(qt)
    return jnp.concatenate([jnp.where(first, qt, zero), jnp.where(first, zero, qt)], axis=0)


def _na_kernel(with_ctx_out, q_ref, k_ref, v_ref, *rest):
    if with_ctx_out:
        qc_ref, kc_ref, vc_ref, ring_ref, mask_ref, o_ref, oc_ref, tab_ref = rest
    else:
        kc_ref, vc_ref, ring_ref, mask_ref, o_ref, tab_ref = rest
    seq = q_ref.shape[1]
    n_tiles = seq // Q_TILE
    first = lax.broadcasted_iota(jnp.int32, (1, PAIR_W), 1) < NA_HEAD_DIM
    kc = kc_ref[0]
    vc = vc_ref[0]
    ctx_parts = range(0, kc.shape[0], PAIR_W)

    @pl.when(pl.program_id(1) == 0)
    def _():
        for hh in range(HEADS_PER_BLOCK):
            def toeplitz(dr, shift):
                rows = jnp.broadcast_to(ring_ref[hh, dr:dr + 1, :], (GRID_W, PAIR_W))
                return pltpu.roll(rows, shift, 1, stride=1, stride_axis=0)
            pair = lambda dr0: jnp.where(first, toeplitz(dr0, 0), toeplitz(dr0 + 1, NA_HEAD_DIM))
            for dr0 in range(N_DR0):
                tab_ref[hh, dr0] = pair(dr0) + mask_ref[0]
            tab_ref[hh, N_DR0] = pair(DR_LO - 1) + mask_ref[1]
            tab_ref[hh, N_DR0 + 1] = pair(DR_HI) + mask_ref[2]

    def tab_index(dr0, clamped):
        if not clamped and dr0 == DR_LO - 1:
            return N_DR0
        if not clamped and dr0 == DR_HI:
            return N_DR0 + 1
        return dr0

    if with_ctx_out:
        n = qc_ref.shape[1]
        (p,), l = _softmax_rows([_dot_nt(_stack_heads(qc_ref[0], first), kc)], [None])
        oc = _dot(p.astype(BF16), vc) * (1.0 / l)
        oc_ref[0] = jnp.where(first, oc[:n], oc[n:]).astype(oc_ref.dtype)

    def scores(q0, k0, win_rows, *_):
        qs = _stack_heads(q_ref[0, pl.ds(q0, Q_TILE), :], first)
        return _dot_nt(qs, k_ref[0, pl.ds(k0, win_rows * GRID_W), :]), _dot_nt(qs, kc)

    def finish(s_loc, s_ctx, q0, k0, win_rows, dr_base, clamped):
        p_loc, p_ctx, inv_l = [], [], []
        zero = jnp.zeros((GRID_W, PAIR_W), BF16)
        for hh in range(HEADS_PER_BLOCK):
            for i in range(Q_TILE_ROWS):
                r0 = hh * Q_TILE + i * GRID_W
                lo, hi = (0, win_rows) if clamped else (i, i + NA_WIN_ROWS)
                pairs = range(lo // 2, (hi + 1) // 2)
                parts = [s_loc[r0:r0 + GRID_W, wp * PAIR_W:(wp + 1) * PAIR_W] for wp in pairs]
                biases = [tab_ref[hh, tab_index(2 * wp - i + dr_base, clamped)] for wp in pairs]
                parts += [s_ctx[r0:r0 + GRID_W, c:c + PAIR_W] for c in ctx_parts]
                biases += [None] * len(ctx_parts)
                p_parts, l = _softmax_rows(parts, biases)
                p_loc.append(jnp.concatenate(
                    [p_parts[pairs.index(wp)].astype(BF16) if wp in pairs else zero
                     for wp in range(win_rows // 2)], axis=-1))
                p_ctx.append(jnp.concatenate([p.astype(BF16) for p in p_parts[len(pairs):]], axis=-1))
                inv_l.append(1.0 / l)
        vw = v_ref[0, pl.ds(k0, win_rows * GRID_W), :]
        o = _dot(jnp.concatenate(p_loc, axis=0), vw) + _dot(jnp.concatenate(p_ctx, axis=0), vc)
        o = o * jnp.concatenate(inv_l, axis=0)
        o_ref[0, pl.ds(q0, Q_TILE), :] = jnp.where(first, o[:Q_TILE], o[Q_TILE:]).astype(o_ref.dtype)

    tiles = [(0, 0, NA_WIN_ROWS, NA_WIN_ROWS - 1, True),
             (seq - Q_TILE, seq - NA_WIN_ROWS * GRID_W, NA_WIN_ROWS, Q_TILE_ROWS - 1, True)]
    tiles += [(t * Q_TILE, t * Q_TILE - (NA_WIN_ROWS // 2) * GRID_W, NA_WIN_ROWS + Q_TILE_ROWS, DR_LO, False)
              for t in range(1, n_tiles - 1)]
    for cfg in tiles:
        finish(*scores(*cfg), *cfg)


def _na_attention(q, k, v, qc, kc, vc, ring):
    b, seq, _ = q.shape
    ctx_len = kc.shape[1]
    lat = pl.BlockSpec((1, seq, PAIR_W), lambda j, i: (i, 0, j))
    ctx = pl.BlockSpec((1, ctx_len, PAIR_W), lambda j, i: (i, 0, j))
    ring_spec = pl.BlockSpec((HEADS_PER_BLOCK,) + ring.shape[1:], lambda j, i: (j, 0, 0))
    masks = _bias_masks()
    with_ctx_out = qc is not None
    lat_shape = jax.ShapeDtypeStruct((b, seq, D_MODEL), BF16)
    ctx_shape = jax.ShapeDtypeStruct((b, ctx_len, D_MODEL), BF16)
    return pl.pallas_call(
        functools.partial(_na_kernel, with_ctx_out),
        grid=(N_PAIRS, b),
        in_specs=[lat, lat, lat] + [ctx] * (3 if with_ctx_out else 2) + [ring_spec, _const_spec(masks.shape)],
        out_specs=[lat, ctx] if with_ctx_out else lat,
        out_shape=[lat_shape, ctx_shape] if with_ctx_out else lat_shape,
        scratch_shapes=[pltpu.VMEM((HEADS_PER_BLOCK, N_DR0 + 2, GRID_W, PAIR_W), F32)],
        compiler_params=_params(2),
        name="na_attention",
    )(q, k, v, *([qc] if with_ctx_out else []), kc, vc, ring, masks)


def _bias_ring(rpb):
    n_dc = 2 * NA_WIN_COLS - 1
    return jnp.concatenate([rpb[..., NA_WIN_COLS - 1:],
                            jnp.zeros(rpb.shape[:2] + (PAIR_W - n_dc,), rpb.dtype),
                            rpb[..., :NA_WIN_COLS - 1]], axis=-1).astype(F32)


def _bias_masks():
    qcol = np.arange(GRID_W)[:, None]
    kcol = np.arange(GRID_W)[None, :]
    cstart = np.clip(qcol - NA_WIN_COLS // 2, 0, GRID_W - NA_WIN_COLS)
    col_ok = np.tile((kcol >= cstart) & (kcol < cstart + NA_WIN_COLS), (1, 2))
    second = np.arange(PAIR_W)[None, :] >= GRID_W
    oks = [col_ok, col_ok & second, col_ok & ~second]
    return jnp.asarray(np.stack([np.where(ok, 0.0, NEG_INF) for ok in oks]), F32)


def _gelu_tanh(x):
    c = float(np.sqrt(2.0 / np.pi))
    return x * (0.5 + 0.5 * jnp.tanh(x * (c + (c * 0.044715) * (x * x))))


def _gmlp_kernel(x_ref, mod_ref, n1g_ref, win_ref, bin_ref, lng_ref, lnb_ref, ws_ref, bs_ref, a_ref):
    mod = mod_ref[0]
    gw = D_MODEL // GMLP_GROUPS

    def layer_norm(v):
        mu = jnp.mean(v, axis=-1, keepdims=True)
        vc = v - mu
        var = jnp.mean(vc * vc, axis=-1, keepdims=True)
        return (vc * lax.rsqrt(var + EPS) * lng_ref[...] + lnb_ref[...]).astype(BF16)

    rows_per_block = x_ref.shape[0] // POST_ROW_BLOCKS
    blocks = [r * rows_per_block for r in range(POST_ROW_BLOCKS)]
    hs = [_norm_mod(x_ref[pl.ds(r0, rows_per_block), :], n1g_ref[...], mod[1:2], mod[0:1]).astype(BF16)
          for r0 in blocks]
    zs = [_dot(h, win_ref[...]) for h in hs]
    zs = [_gelu_tanh(z + bin_ref[...]) for z in zs]
    vns = [layer_norm(z[:, D_MODEL:]) for z in zs]
    for r0, z, vn in zip(blocks, zs, vns):
        for n in range(rows_per_block // GMLP_CHUNK):
            rows = slice(n * GMLP_CHUNK, (n + 1) * GMLP_CHUNK)
            for g in range(GMLP_GROUPS):
                cols = slice(g * gw, (g + 1) * gw)
                mixed = _dot(ws_ref[g], vn[rows, cols]) + bs_ref[g]
                a_ref[pl.ds(r0 + n * GMLP_CHUNK, GMLP_CHUNK), cols] = (
                    z[rows, cols] * mixed).astype(a_ref.dtype)


def _gmlp(x, mod, rows_per_mod, n1g, w_in, b_in, ln_g, ln_b, w_s, b_s):
    t = x.shape[0]
    tok = lambda i: (i, 0)
    return pl.pallas_call(
        _gmlp_kernel,
        grid=(t // TM,),
        in_specs=[
            pl.BlockSpec((TM, D_MODEL), tok),
            pl.BlockSpec((1, 6, D_MODEL), lambda i: (i * TM // rows_per_mod, 0, 0)),
            _const_spec((1, D_MODEL)),
            _const_spec(w_in.shape),
            _const_spec(b_in.shape),
            _const_spec((1, D_MODEL)),
            _const_spec((1, D_MODEL)),
            _const_spec(w_s.shape),
            _const_spec(b_s.shape),
        ],
        out_specs=pl.BlockSpec((TM, D_MODEL), tok),
        out_shape=jax.ShapeDtypeStruct((t, D_MODEL), BF16),
        compiler_params=_params(),
        name="gmlp",
    )(x, mod, n1g, w_in, b_in, ln_g, ln_b, w_s, b_s)


def _pool_kernel(seq_len, x_ref, xp_ref, xn_ref, mod_ref, n1g_ref, a_ref, hbuf, sbuf):
    tm = x_ref.shape[0]
    tiles_per_seq = seq_len // tm
    pos = pl.program_id(0) % tiles_per_seq
    mod = mod_ref[0]
    n1g = n1g_ref[...]
    norm = lambda x: _norm_mod(x, n1g, mod[1:2], mod[0:1])
    zero = jnp.zeros((POOL_HALO, D_MODEL), F32)
    h0 = POOL_HALO
    chunks = lambda n: [(r, min(POOL_CHUNK, n - r)) for r in range(0, n, POOL_CHUNK)]
    hbuf[0:h0] = jnp.where(pos > 0, norm(xp_ref[...]), zero)
    for r, n in chunks(tm):
        hbuf[h0 + r:h0 + r + n] = norm(x_ref[r:r + n])
    hbuf[h0 + tm:h0 + tm + POOL_HALO] = jnp.where(pos < tiles_per_seq - 1, norm(xn_ref[...]), zero)
    hbuf[h0 + tm + POOL_HALO:] = jnp.zeros((POOL_PAD, D_MODEL), F32)
    for g, w in enumerate(POOL_WINDOWS):
        cols = slice(g * POOL_GROUP, (g + 1) * POOL_GROUP)
        n_levels = g
        src = lambda lo, hi: hbuf[lo:hi, cols]
        for k in range(n_levels):
            step = 2 ** k
            for r, n in chunks(tm + POOL_HALO * (n_levels - k)):
                sbuf[k, r:r + n] = src(r, r + n) + src(r + step, r + n + step)
            src = functools.partial(lambda k, lo, hi: sbuf[k, lo:hi], k)
        span = 2 ** n_levels
        lo = h0 - w // 2
        for r, n in chunks(tm):
            total = functools.reduce(jnp.add, [src(lo + r + j, lo + r + j + n) for j in range(0, w, span)])
            t = pos * tm + r + lax.broadcasted_iota(jnp.int32, (n, 1), 0)
            cnt = jnp.minimum(t - w // 2 + w, seq_len) - jnp.maximum(t - w // 2, 0)
            mean = total * (1.0 / cnt.astype(F32))
            a_ref[r:r + n, cols] = (mean - hbuf[h0 + r:h0 + r + n, cols]).astype(a_ref.dtype)


def _pool(x, seq_len, tm, mod, rows_per_mod, n1g):
    t = x.shape[0]
    hb = tm // POOL_HALO
    n_halo = t // POOL_HALO
    n_levels = len(POOL_WINDOWS) - 1
    return pl.pallas_call(
        functools.partial(_pool_kernel, seq_len),
        grid=(t // tm,),
        in_specs=[
            pl.BlockSpec((tm, D_MODEL), lambda i: (i, 0)),
            pl.BlockSpec((POOL_HALO, D_MODEL), lambda i: (jnp.maximum(i * hb - 1, 0), 0)),
            pl.BlockSpec((POOL_HALO, D_MODEL), lambda i: (jnp.minimum((i + 1) * hb, n_halo - 1), 0)),
            pl.BlockSpec((1, 6, D_MODEL), lambda i: (i * tm // rows_per_mod, 0, 0)),
            _const_spec((1, D_MODEL)),
        ],
        out_specs=pl.BlockSpec((tm, D_MODEL), lambda i: (i, 0)),
        out_shape=jax.ShapeDtypeStruct((t, D_MODEL), BF16),
        scratch_shapes=[
            pltpu.VMEM((tm + 2 * POOL_HALO + POOL_PAD, D_MODEL), F32),
            pltpu.VMEM((n_levels, tm + POOL_HALO * n_levels, POOL_GROUP), F32),
        ],
        compiler_params=_params(),
        name="pool",
    )(x, x, x, mod, n1g)


def kernel(x, c, ctx, c_ctx, ada_w, ada_b, norm1_g, norm2_g, ffn_w_gate, ffn_w_up, ffn_w_down, na_w_qkv, na_w_o, na_q_norm, na_k_norm, na_rpb, gm_w_in, gm_b_in, gm_ln_g, gm_ln_b, gm_w_s, gm_b_s, gm_w_out, pool_w, pool_scale):
    b, seq, d = x.shape
    ctx_len = ctx.shape[1]
    assert d == D_MODEL and seq % TM == 0 and (b * ctx_len) % TM == 0 and seq % (Q_TILE * 2) == 0

    c_all = jnp.zeros((MOD_ROWS, d), F32).at[:b].set(c).at[b].set(c_ctx)
    mods = _ada_mods(c_all, ada_w, ada_b)

    heads_per_seg = 4
    seg = jnp.asarray(np.kron(np.eye(heads_per_seg), np.ones((NA_HEAD_DIM, NA_HEAD_DIM))), BF16)
    row = lambda v: v.reshape(1, d)
    tile_gain = lambda g: jnp.tile(g, NA_HEADS).reshape(1, d)

    wg, wu, wd = (w.astype(BF16) for w in (ffn_w_gate, ffn_w_up, ffn_w_down))
    xl = x.reshape(b * seq, d)
    xc = ctx.reshape(b * ctx_len, d)
    for i in range(DEPTH):
        kind, j = i % N_MIXERS, i // N_MIXERS
        last = i == DEPTH - 1
        need_ctx = (not last) or kind == 0
        mod_l = mods[i, :b].reshape(b, 6, d)
        mod_c = mods[i, b:b + 1].reshape(1, 6, d)
        streams = [(xl, mod_l, seq, seq)]
        if need_ctx:
            streams.append((xc, mod_c, b * ctx_len, ctx_len))
        n1g, n2g = row(norm1_g[i]), row(norm2_g[i])

        pool_s = None
        if kind == 0:
            w_qkv = na_w_qkv[j].astype(BF16)
            qg, kg = tile_gain(na_q_norm[j]), tile_gain(na_k_norm[j])
            (ql, kl, vl), (qc, kc, vc) = (
                tuple(t.reshape(b, -1, d) for t in _qkv(xs, m, rpm, n1g, w_qkv, qg, kg, seg))
                for xs, m, rpm, _ in streams)
            acts = _na_attention(ql, kl, vl, None if last else qc, kc, vc, _bias_ring(na_rpb[j]))
            acts = [a.reshape(-1, d) for a in (acts if isinstance(acts, (list, tuple)) else [acts])]
            mix_w = na_w_o[j].astype(BF16)
        elif kind == 1:
            w_in = gm_w_in[j].astype(BF16)
            b_in = gm_b_in[j].reshape(1, -1)
            w_s = gm_w_s[j].astype(BF16)
            b_s = jnp.broadcast_to(gm_b_s[j][:, :, None], (GMLP_GROUPS, GMLP_CHUNK, d // GMLP_GROUPS))
            acts = [_gmlp(xs, m, rpm, n1g, w_in, b_in, row(gm_ln_g[j]), row(gm_ln_b[j]), w_s, b_s)
                    for xs, m, rpm, _ in streams]
            mix_w = gm_w_out[j].astype(BF16)
        else:
            acts = [_pool(xs, sl, min(sl, POOL_TM), m, rpm, n1g) for xs, m, rpm, sl in streams]
            mix_w = pool_w[j].astype(BF16)
            pool_s = row(pool_scale[j])

        outs = [_post(a, xs, m, rpm, n2g, mix_w, pool_s, i, wg, wu, wd)
                for a, (xs, m, rpm, _) in zip(acts, streams)]
        xl = outs[0]
        if not last:
            xc = outs[1]
    return xl.reshape(b, seq, d)
```

```python
import functools

import numpy as np
import jax
import jax.numpy as jnp
from jax import lax
from jax.experimental import pallas as pl
from jax.experimental.pallas import tpu as pltpu

D_MODEL = 1024
DEPTH = 4
GRID_W = 64
N_MIXERS = 3
EPS = 1e-6
NEG_INF = -1e30

NA_HEADS = 16
NA_HEAD_DIM = D_MODEL // NA_HEADS
NA_WIN_ROWS = 8
NA_WIN_COLS = 16
HEADS_PER_BLOCK = 2
PAIR_W = HEADS_PER_BLOCK * NA_HEAD_DIM
N_PAIRS = NA_HEADS // HEADS_PER_BLOCK
Q_TILE_ROWS = 4
Q_TILE = Q_TILE_ROWS * GRID_W
NA_TILE_GROUP = 1
NA_BATCH_BLOCK = 2
N_DR0 = 2 * NA_WIN_ROWS - 2
DR_LO = NA_WIN_ROWS - 1 - NA_WIN_ROWS // 2
DR_HI = DR_LO + NA_WIN_ROWS - 1

GMLP_CHUNK = 128
GMLP_GROUPS = 8
POOL_WINDOWS = (2, 4, 8, 16)
POOL_GROUP = D_MODEL // len(POOL_WINDOWS)
POOL_HALO = 8
POOL_PAD = 16
POOL_CHUNK = 32

MOD_ROWS = 32
ADA_TN = 1024
TM = 1024
ROW_BLOCK = 256
POST_TM = 1024
POOL_TM = 512
VMEM_LIMIT_BYTES = 56 * 1024 * 1024

BF16 = jnp.bfloat16
F32 = jnp.float32


def _dot(a, b):
    return jnp.dot(a, b, preferred_element_type=F32)


def _dot_nt(a, b):
    return lax.dot_general(a, b, (((1,), (1,)), ((), ())), preferred_element_type=F32)


def _silu(x):
    return x / (1.0 + jnp.exp(-x))


def _const_spec(shape):
    nd = len(shape)
    return pl.BlockSpec(shape, lambda *_: (0,) * nd, pipeline_mode=pl.Buffered(1))


def _layer_spec(shape, layer):
    nd = len(shape)
    return pl.BlockSpec((1,) + tuple(shape[1:]), lambda *_: (layer,) + (0,) * (nd - 1),
                        pipeline_mode=pl.Buffered(1))


def _params(n_axes=1):
    return pltpu.CompilerParams(
        dimension_semantics=("arbitrary",) * n_axes, vmem_limit_bytes=VMEM_LIMIT_BYTES)


def _norm_mod(x, g, sc, sh):
    ms = jnp.mean(x * x, axis=-1, keepdims=True)
    return (x * lax.rsqrt(ms + EPS)) * (g * (1.0 + sc)) + sh


def _ada_kernel(c_ref, w_ref, b_ref, o_ref):
    s = _silu(c_ref[...]).astype(BF16)
    o_ref[0] = _dot(s, w_ref[0].astype(BF16)) + b_ref[0]


def _ada_mods(c_all, ada_w, ada_b):
    n = ada_w.shape[-1]
    return pl.pallas_call(
        _ada_kernel,
        grid=(DEPTH, n // ADA_TN),
        in_specs=[
            pl.BlockSpec((MOD_ROWS, D_MODEL), lambda i, j: (0, 0)),
            pl.BlockSpec((1, D_MODEL, ADA_TN), lambda i, j: (i, 0, j)),
            pl.BlockSpec((1, 1, ADA_TN), lambda i, j: (i, 0, j)),
        ],
        out_specs=pl.BlockSpec((1, MOD_ROWS, ADA_TN), lambda i, j: (i, 0, j)),
        out_shape=jax.ShapeDtypeStruct((DEPTH, MOD_ROWS, n), F32),
        compiler_params=_params(2),
        name="ada_mods",
    )(c_all, ada_w, ada_b.reshape(DEPTH, 1, n))


def _post_kernel(pooled, a_ref, x_ref, mod_ref, n2g_ref, *rest):
    if pooled:
        wp_ref, ps_ref, wg_ref, wu_ref, wd_ref, o_ref = rest
    else:
        wm_ref, wg_ref, wu_ref, wd_ref, o_ref = rest
    mod = mod_ref[0]
    g1, sh2, sc2, g2 = mod[2:3], mod[3:4], mod[4:5], mod[5:6]
    blocks = [pl.ds(r, ROW_BLOCK) for r in range(0, x_ref.shape[0], ROW_BLOCK)]

    def mix(rows):
        a = a_ref[rows, :]
        if pooled:
            return jnp.concatenate(
                [_dot(a[:, g * POOL_GROUP:(g + 1) * POOL_GROUP], wp_ref[g])
                 for g in range(len(POOL_WINDOWS))], axis=-1) * ps_ref[...]
        return _dot(a, wm_ref[...])

    ys = [mix(rows) for rows in blocks]
    x1s = [x_ref[rows, :] + g1 * y for rows, y in zip(blocks, ys)]
    h2s = [_norm_mod(x1, n2g_ref[...], sc2, sh2).astype(BF16) for x1 in x1s]
    gus = [(_dot(h2, wg_ref[0]), _dot(h2, wu_ref[0])) for h2 in h2s]
    acts = [(_silu(gate) * up).astype(BF16) for gate, up in gus]
    for rows, x1, act in zip(blocks, x1s, acts):
        o_ref[rows, :] = x1 + g2 * _dot(act, wd_ref[0])


def _post(a, x, mod, rows_per_mod, n2g, mix_w, pool_scale, layer, wg, wu, wd):
    t = x.shape[0]
    pooled = pool_scale is not None
    tok = lambda i: (i, 0)
    in_specs = [
        pl.BlockSpec((POST_TM, D_MODEL), tok),
        pl.BlockSpec((POST_TM, D_MODEL), tok),
        pl.BlockSpec((1, 6, D_MODEL), lambda i: (i * POST_TM // rows_per_mod, 0, 0)),
        _const_spec((1, D_MODEL)),
        _const_spec(mix_w.shape),
    ]
    args = [a, x, mod, n2g, mix_w]
    if pooled:
        in_specs.append(_const_spec((1, D_MODEL)))
        args.append(pool_scale)
    in_specs += [_layer_spec(w.shape, layer) for w in (wg, wu, wd)]
    args += [wg, wu, wd]
    return pl.pallas_call(
        functools.partial(_post_kernel, pooled),
        grid=(t // POST_TM,),
        in_specs=in_specs,
        out_specs=pl.BlockSpec((POST_TM, D_MODEL), tok),
        out_shape=jax.ShapeDtypeStruct((t, D_MODEL), F32),
        compiler_params=_params(),
        name="post_pool" if pooled else "post_dense",
    )(*args)


def _qkv_kernel(x_ref, mod_ref, n1g_ref, w_ref, qg_ref, kg_ref, seg_ref, q_ref, k_ref, v_ref):
    mod = mod_ref[0]
    seg = seg_ref[...]
    seg_w = seg.shape[0]

    def mean_sq(t):
        return jnp.concatenate(
            [_dot((t[:, c:c + seg_w] * t[:, c:c + seg_w]).astype(BF16), seg)
             for c in range(0, D_MODEL, seg_w)], axis=-1) * (1.0 / NA_HEAD_DIM)

    blocks = [pl.ds(r, ROW_BLOCK) for r in range(0, x_ref.shape[0], ROW_BLOCK)]
    hs = [_norm_mod(x_ref[rows, :], n1g_ref[...], mod[1:2], mod[0:1]).astype(BF16) for rows in blocks]
    qkvs = [_dot(h, w_ref[...]) for h in hs]
    for rows, qkv in zip(blocks, qkvs):
        v_ref[rows, :] = qkv[:, 2 * D_MODEL:].astype(BF16)
    qs = [qkv[:, :D_MODEL] for qkv in qkvs]
    ks = [qkv[:, D_MODEL:2 * D_MODEL] for qkv in qkvs]
    q_ms = [mean_sq(q) for q in qs]
    k_ms = [mean_sq(k) for k in ks]
    for rows, q, k, qm, km in zip(blocks, qs, ks, q_ms, k_ms):
        q_ref[rows, :] = (q * lax.rsqrt(qm + EPS) * qg_ref[...] * (NA_HEAD_DIM ** -0.5)).astype(BF16)
        k_ref[rows, :] = (k * lax.rsqrt(km + EPS) * kg_ref[...]).astype(BF16)


def _qkv(x, mod, rows_per_mod, n1g, w_qkv, qg, kg, seg):
    t = x.shape[0]
    tok = lambda i: (i, 0)
    out = jax.ShapeDtypeStruct((t, D_MODEL), BF16)
    return pl.pallas_call(
        _qkv_kernel,
        grid=(t // TM,),
        in_specs=[
            pl.BlockSpec((TM, D_MODEL), tok),
            pl.BlockSpec((1, 6, D_MODEL), lambda i: (i * TM // rows_per_mod, 0, 0)),
            _const_spec((1, D_MODEL)),
            _const_spec(w_qkv.shape),
            _const_spec((1, D_MODEL)),
            _const_spec((1, D_MODEL)),
            _const_spec(seg.shape),
        ],
        out_specs=[pl.BlockSpec((TM, D_MODEL), tok)] * 3,
        out_shape=[out, out, out],
        compiler_params=_params(),
        name="qkv",
    )(x, mod, n1g, w_qkv, qg, kg, seg)


def _softmax_rows(s_parts, biases):
    biased = [s if b is None else s + b for s, b in zip(s_parts, biases)]
    m = jnp.max(functools.reduce(jnp.maximum, biased), axis=-1, keepdims=True)
    p_parts = [jnp.exp(s - m) for s in biased]
    l = jnp.sum(functools.reduce(jnp.add, p_parts), axis=-1, keepdims=True)
    return p_parts, l


def _stack_heads(qt, first):
    zero = jnp.zeros_like(qt)
    return jnp.concatenate([jnp.where(first, qt, zero), jnp.where(first, zero, qt)], axis=0)


def _na_kernel(with_ctx_out, q_ref, k_ref, v_ref, *rest):
    if with_ctx_out:
        qc_ref, kc_ref, vc_ref, ring_ref, mask_ref, o_ref, oc_ref, tab_ref = rest
    else:
        kc_ref, vc_ref, ring_ref, mask_ref, o_ref, tab_ref = rest
    first = lax.broadcasted_iota(jnp.int32, (1, PAIR_W), 1) < NA_HEAD_DIM

    @pl.when(pl.program_id(1) == 0)
    def _():
        for hh in range(HEADS_PER_BLOCK):
            def toeplitz(dr, shift):
                rows = jnp.broadcast_to(ring_ref[hh, dr:dr + 1, :], (GRID_W, PAIR_W))
                return pltpu.roll(rows, shift, 1, stride=1, stride_axis=0)
            pair = lambda dr0: jnp.where(first, toeplitz(dr0, 0), toeplitz(dr0 + 1, NA_HEAD_DIM))
            for dr0 in range(N_DR0):
                tab_ref[hh, dr0] = pair(dr0) + mask_ref[0]
            tab_ref[hh, N_DR0] = pair(DR_LO - 1) + mask_ref[1]
            tab_ref[hh, N_DR0 + 1] = pair(DR_HI) + mask_ref[2]

    for bb in range(q_ref.shape[0]):
        _na_batch(first, tab_ref, q_ref.at[bb], k_ref.at[bb], v_ref.at[bb],
                  qc_ref.at[bb] if with_ctx_out else None, kc_ref.at[bb], vc_ref.at[bb],
                  o_ref.at[bb], oc_ref.at[bb] if with_ctx_out else None)


def _na_batch(first, tab_ref, q_ref, k_ref, v_ref, qc_ref, kc_ref, vc_ref, o_ref, oc_ref):
    seq = q_ref.shape[0]
    n_tiles = seq // Q_TILE
    kc = kc_ref[...]
    vc = vc_ref[...]
    ctx_parts = range(0, kc.shape[0], PAIR_W)

    def tab_index(dr0, clamped):
        if not clamped and dr0 == DR_LO - 1:
            return N_DR0
        if not clamped and dr0 == DR_HI:
            return N_DR0 + 1
        return dr0

    if qc_ref is not None:
        n = qc_ref.shape[0]
        (p,), l = _softmax_rows([_dot_nt(_stack_heads(qc_ref[...], first), kc)], [None])
        oc = _dot(p.astype(BF16), vc) * (1.0 / l)
        oc_ref[...] = jnp.where(first, oc[:n], oc[n:]).astype(oc_ref.dtype)

    def scores(q0, k0, win_rows, *_):
        qs = _stack_heads(q_ref[pl.ds(q0, Q_TILE), :], first)
        return _dot_nt(qs, k_ref[pl.ds(k0, win_rows * GRID_W), :]), _dot_nt(qs, kc)

    def probs(s_loc, s_ctx, q0, k0, win_rows, dr_base, clamped):
        p_loc, p_ctx, inv_l = [], [], []
        zero = jnp.zeros((GRID_W, PAIR_W), BF16)
        for hh in range(HEADS_PER_BLOCK):
            for i in range(Q_TILE_ROWS):
                r0 = hh * Q_TILE + i * GRID_W
                lo, hi = (0, win_rows) if clamped else (i, i + NA_WIN_ROWS)
                pairs = range(lo // 2, (hi + 1) // 2)
                parts = [s_loc[r0:r0 + GRID_W, wp * PAIR_W:(wp + 1) * PAIR_W] for wp in pairs]
                biases = [tab_ref[hh, tab_index(2 * wp - i + dr_base, clamped)] for wp in pairs]
                parts += [s_ctx[r0:r0 + GRID_W, c:c + PAIR_W] for c in ctx_parts]
                biases += [None] * len(ctx_parts)
                p_parts, l = _softmax_rows(parts, biases)
                p_loc.append(jnp.concatenate(
                    [p_parts[pairs.index(wp)].astype(BF16) if wp in pairs else zero
                     for wp in range(win_rows // 2)], axis=-1))
                p_ctx.append(jnp.concatenate([p.astype(BF16) for p in p_parts[len(pairs):]], axis=-1))
                inv_l.append(1.0 / l)
        return (jnp.concatenate(p_loc, axis=0), jnp.concatenate(p_ctx, axis=0),
                jnp.concatenate(inv_l, axis=0))

    def output(p_loc, p_ctx, inv_l, q0, k0, win_rows, *_):
        o = (_dot(p_loc, v_ref[pl.ds(k0, win_rows * GRID_W), :]) + _dot(p_ctx, vc)) * inv_l
        o_ref[pl.ds(q0, Q_TILE), :] = jnp.where(first, o[:Q_TILE], o[Q_TILE:]).astype(o_ref.dtype)

    tiles = [(0, 0, NA_WIN_ROWS, NA_WIN_ROWS - 1, True),
             (seq - Q_TILE, seq - NA_WIN_ROWS * GRID_W, NA_WIN_ROWS, Q_TILE_ROWS - 1, True)]
    tiles += [(t * Q_TILE, t * Q_TILE - (NA_WIN_ROWS // 2) * GRID_W, NA_WIN_ROWS + Q_TILE_ROWS, DR_LO, False)
              for t in range(1, n_tiles - 1)]
    for g0 in range(0, len(tiles), NA_TILE_GROUP):
        group = tiles[g0:g0 + NA_TILE_GROUP]
        ss = [scores(*cfg) for cfg in group]
        ps = [probs(*s, *cfg) for s, cfg in zip(ss, group)]
        for p, cfg in zip(ps, group):
            output(*p, *cfg)


def _na_attention(q, k, v, qc, kc, vc, ring):
    b, seq, _ = q.shape
    ctx_len = kc.shape[1]
    lat = pl.BlockSpec((NA_BATCH_BLOCK, seq, PAIR_W), lambda j, i: (i, 0, j))
    ctx = pl.BlockSpec((NA_BATCH_BLOCK, ctx_len, PAIR_W), lambda j, i: (i, 0, j))
    ring_spec = pl.BlockSpec((HEADS_PER_BLOCK,) + ring.shape[1:], lambda j, i: (j, 0, 0))
    masks = _bias_masks()
    with_ctx_out = qc is not None
    lat_shape = jax.ShapeDtypeStruct((b, seq, D_MODEL), BF16)
    ctx_shape = jax.ShapeDtypeStruct((b, ctx_len, D_MODEL), BF16)
    return pl.pallas_call(
        functools.partial(_na_kernel, with_ctx_out),
        grid=(N_PAIRS, b // NA_BATCH_BLOCK),
        in_specs=[lat, lat, lat] + [ctx] * (3 if with_ctx_out else 2) + [ring_spec, _const_spec(masks.shape)],
        out_specs=[lat, ctx] if with_ctx_out else lat,
        out_shape=[lat_shape, ctx_shape] if with_ctx_out else lat_shape,
        scratch_shapes=[pltpu.VMEM((HEADS_PER_BLOCK, N_DR0 + 2, GRID_W, PAIR_W), F32)],
        compiler_params=_params(2),
        name="na_attention",
    )(q, k, v, *([qc] if with_ctx_out else []), kc, vc, ring, masks)


def _bias_ring(rpb):
    n_dc = 2 * NA_WIN_COLS - 1
    return jnp.concatenate([rpb[..., NA_WIN_COLS - 1:],
                            jnp.zeros(rpb.shape[:2] + (PAIR_W - n_dc,), rpb.dtype),
                            rpb[..., :NA_WIN_COLS - 1]], axis=-1).astype(F32)


def _bias_masks():
    qcol = np.arange(GRID_W)[:, None]
    kcol = np.arange(GRID_W)[None, :]
    cstart = np.clip(qcol - NA_WIN_COLS // 2, 0, GRID_W - NA_WIN_COLS)
    col_ok = np.tile((kcol >= cstart) & (kcol < cstart + NA_WIN_COLS), (1, 2))
    second = np.arange(PAIR_W)[None, :] >= GRID_W
    oks = [col_ok, col_ok & second, col_ok & ~second]
    return jnp.asarray(np.stack([np.where(ok, 0.0, NEG_INF) for ok in oks]), F32)


def _gelu_tanh(x):
    c = float(np.sqrt(2.0 / np.pi))
    return x * (0.5 + 0.5 * jnp.tanh(x * (c + (c * 0.044715) * (x * x))))


def _gmlp_kernel(x_ref, mod_ref, n1g_ref, win_ref, bin_ref, lng_ref, lnb_ref, ws_ref, bs_ref, a_ref):
    mod = mod_ref[0]
    gw = D_MODEL // GMLP_GROUPS

    def layer_norm(v):
        mu = jnp.mean(v, axis=-1, keepdims=True)
        vc = v - mu
        var = jnp.mean(vc * vc, axis=-1, keepdims=True)
        return (vc * lax.rsqrt(var + EPS) * lng_ref[...] + lnb_ref[...]).astype(BF16)

    rows_per_block = ROW_BLOCK
    blocks = list(range(0, x_ref.shape[0], ROW_BLOCK))
    hs = [_norm_mod(x_ref[pl.ds(r0, rows_per_block), :], n1g_ref[...], mod[1:2], mod[0:1]).astype(BF16)
          for r0 in blocks]
    zs = [_dot(h, win_ref[...]) for h in hs]
    zs = [_gelu_tanh(z + bin_ref[...]) for z in zs]
    vns = [layer_norm(z[:, D_MODEL:]) for z in zs]
    chunks = [slice(n * GMLP_CHUNK, (n + 1) * GMLP_CHUNK) for n in range(rows_per_block // GMLP_CHUNK)]
    for r0, z, vn in zip(blocks, zs, vns):
        for g in range(GMLP_GROUPS):
            cols = slice(g * gw, (g + 1) * gw)
            mixed = _dot(ws_ref[g], jnp.concatenate([vn[rows, cols] for rows in chunks], axis=1))
            for n, rows in enumerate(chunks):
                a_ref[pl.ds(r0 + rows.start, GMLP_CHUNK), cols] = (
                    z[rows, cols] * (mixed[:, n * gw:(n + 1) * gw] + bs_ref[g])).astype(a_ref.dtype)


def _gmlp(x, mod, rows_per_mod, n1g, w_in, b_in, ln_g, ln_b, w_s, b_s):
    t = x.shape[0]
    tok = lambda i: (i, 0)
    return pl.pallas_call(
        _gmlp_kernel,
        grid=(t // TM,),
        in_specs=[
            pl.BlockSpec((TM, D_MODEL), tok),
            pl.BlockSpec((1, 6, D_MODEL), lambda i: (i * TM // rows_per_mod, 0, 0)),
            _const_spec((1, D_MODEL)),
            _const_spec(w_in.shape),
            _const_spec(b_in.shape),
            _const_spec((1, D_MODEL)),
            _const_spec((1, D_MODEL)),
            _const_spec(w_s.shape),
            _const_spec(b_s.shape),
        ],
        out_specs=pl.BlockSpec((TM, D_MODEL), tok),
        out_shape=jax.ShapeDtypeStruct((t, D_MODEL), BF16),
        compiler_params=_params(),
        name="gmlp",
    )(x, mod, n1g, w_in, b_in, ln_g, ln_b, w_s, b_s)


def _pool_kernel(seq_len, x_ref, xp_ref, xn_ref, mod_ref, n1g_ref, a_ref, hbuf, sbuf):
    tm = x_ref.shape[0]
    tiles_per_seq = seq_len // tm
    pos = pl.program_id(0) % tiles_per_seq
    mod = mod_ref[0]
    n1g = n1g_ref[...]
    norm = lambda x: _norm_mod(x, n1g, mod[1:2], mod[0:1])
    zero = jnp.zeros((POOL_HALO, D_MODEL), F32)
    h0 = POOL_HALO
    chunks = lambda n: [(r, min(POOL_CHUNK, n - r)) for r in range(0, n, POOL_CHUNK)]
    hbuf[0:h0] = jnp.where(pos > 0, norm(xp_ref[...]), zero)
    for r, n in chunks(tm):
        hbuf[h0 + r:h0 + r + n] = norm(x_ref[r:r + n])
    hbuf[h0 + tm:h0 + tm + POOL_HALO] = jnp.where(pos < tiles_per_seq - 1, norm(xn_ref[...]), zero)
    hbuf[h0 + tm + POOL_HALO:] = jnp.zeros((POOL_PAD, D_MODEL), F32)
    for g, w in enumerate(POOL_WINDOWS):
        cols = slice(g * POOL_GROUP, (g + 1) * POOL_GROUP)
        n_levels = g
        src = lambda lo, hi: hbuf[lo:hi, cols]
        for k in range(n_levels):
            step = 2 ** k
            for r, n in chunks(tm + POOL_HALO * (n_levels - k)):
                sbuf[k, r:r + n] = src(r, r + n) + src(r + step, r + n + step)
            src = functools.partial(lambda k, lo, hi: sbuf[k, lo:hi], k)
        span = 2 ** n_levels
        lo = h0 - w // 2
        for r, n in chunks(tm):
            total = functools.reduce(jnp.add, [src(lo + r + j, lo + r + j + n) for j in range(0, w, span)])
            t = pos * tm + r + lax.broadcasted_iota(jnp.int32, (n, 1), 0)
            cnt = jnp.minimum(t - w // 2 + w, seq_len) - jnp.maximum(t - w // 2, 0)
            mean = total * (1.0 / cnt.astype(F32))
            a_ref[r:r + n, cols] = (mean - hbuf[h0 + r:h0 + r + n, cols]).astype(a_ref.dtype)


def _pool(x, seq_len, tm, mod, rows_per_mod, n1g):
    t = x.shape[0]
    hb = tm // POOL_HALO
    n_halo = t // POOL_HALO
    n_levels = len(POOL_WINDOWS) - 1
    return pl.pallas_call(
        functools.partial(_pool_kernel, seq_len),
        grid=(t // tm,),
        in_specs=[
            pl.BlockSpec((tm, D_MODEL), lambda i: (i, 0)),
            pl.BlockSpec((POOL_HALO, D_MODEL), lambda i: (jnp.maximum(i * hb - 1, 0), 0)),
            pl.BlockSpec((POOL_HALO, D_MODEL), lambda i: (jnp.minimum((i + 1) * hb, n_halo - 1), 0)),
            pl.BlockSpec((1, 6, D_MODEL), lambda i: (i * tm // rows_per_mod, 0, 0)),
            _const_spec((1, D_MODEL)),
        ],
        out_specs=pl.BlockSpec((tm, D_MODEL), lambda i: (i, 0)),
        out_shape=jax.ShapeDtypeStruct((t, D_MODEL), BF16),
        scratch_shapes=[
            pltpu.VMEM((tm + 2 * POOL_HALO + POOL_PAD, D_MODEL), F32),
            pltpu.VMEM((n_levels, tm + POOL_HALO * n_levels, POOL_GROUP), F32),
        ],
        compiler_params=_params(),
        name="pool",
    )(x, x, x, mod, n1g)


def kernel(x, c, ctx, c_ctx, ada_w, ada_b, norm1_g, norm2_g, ffn_w_gate, ffn_w_up, ffn_w_down, na_w_qkv, na_w_o, na_q_norm, na_k_norm, na_rpb, gm_w_in, gm_b_in, gm_ln_g, gm_ln_b, gm_w_s, gm_b_s, gm_w_out, pool_w, pool_scale):
    b, seq, d = x.shape
    ctx_len = ctx.shape[1]
    assert d == D_MODEL and seq % POST_TM == 0 and (b * ctx_len) % POST_TM == 0 and seq % (Q_TILE * 2) == 0

    c_all = jnp.zeros((MOD_ROWS, d), F32).at[:b].set(c).at[b].set(c_ctx)
    mods = _ada_mods(c_all, ada_w, ada_b)

    heads_per_seg = 4
    seg = jnp.asarray(np.kron(np.eye(heads_per_seg), np.ones((NA_HEAD_DIM, NA_HEAD_DIM))), BF16)
    row = lambda v: v.reshape(1, d)
    tile_gain = lambda g: jnp.tile(g, NA_HEADS).reshape(1, d)

    wg, wu, wd = (w.astype(BF16) for w in (ffn_w_gate, ffn_w_up, ffn_w_down))
    xl = x.reshape(b * seq, d)
    xc = ctx.reshape(b * ctx_len, d)
    for i in range(DEPTH):
        kind, j = i % N_MIXERS, i // N_MIXERS
        last = i == DEPTH - 1
        need_ctx = (not last) or kind == 0
        mod_l = mods[i, :b].reshape(b, 6, d)
        mod_c = mods[i, b:b + 1].reshape(1, 6, d)
        streams = [(xl, mod_l, seq, seq)]
        if need_ctx:
            streams.append((xc, mod_c, b * ctx_len, ctx_len))
        n1g, n2g = row(norm1_g[i]), row(norm2_g[i])

        pool_s = None
        if kind == 0:
            w_qkv = na_w_qkv[j].astype(BF16)
            qg, kg = tile_gain(na_q_norm[j]), tile_gain(na_k_norm[j])
            (ql, kl, vl), (qc, kc, vc) = (
                tuple(t.reshape(b, -1, d) for t in _qkv(xs, m, rpm, n1g, w_qkv, qg, kg, seg))
                for xs, m, rpm, _ in streams)
            acts = _na_attention(ql, kl, vl, None if last else qc, kc, vc, _bias_ring(na_rpb[j]))
            acts = [a.reshape(-1, d) for a in (acts if isinstance(acts, (list, tuple)) else [acts])]
            mix_w = na_w_o[j].astype(BF16)
        elif kind == 1:
            w_in = gm_w_in[j].astype(BF16)
            b_in = gm_b_in[j].reshape(1, -1)
            w_s = gm_w_s[j].astype(BF16)
            b_s = jnp.broadcast_to(gm_b_s[j][:, :, None], (GMLP_GROUPS, GMLP_CHUNK, d // GMLP_GROUPS))
            acts = [_gmlp(xs, m, rpm, n1g, w_in, b_in, row(gm_ln_g[j]), row(gm_ln_b[j]), w_s, b_s)
                    for xs, m, rpm, _ in streams]
            mix_w = gm_w_out[j].astype(BF16)
        else:
            acts = [_pool(xs, sl, min(sl, POOL_TM), m, rpm, n1g) for xs, m, rpm, sl in streams]
            mix_w = pool_w[j].astype(BF16)
            pool_s = row(pool_scale[j])

        outs = [_post(a, xs, m, rpm, n2g, mix_w, pool_s, i, wg, wu, wd)
                for a, (xs, m, rpm, _) in zip(acts, streams)]
        xl = outs[0]
        if not last:
            xc = outs[1]
    return xl.reshape(b, seq, d)
```

```python
import functools

import numpy as np
import jax
import jax.numpy as jnp
from jax import lax
from jax.experimental import pallas as pl
from jax.experimental.pallas import tpu as pltpu

D_MODEL = 1024
DEPTH = 4
GRID_W = 64
N_MIXERS = 3
EPS = 1e-6
NEG_INF = -1e30

NA_HEADS = 16
NA_HEAD_DIM = D_MODEL // NA_HEADS
NA_WIN_ROWS = 8
NA_WIN_COLS = 16
HEADS_PER_BLOCK = 2
PAIR_W = HEADS_PER_BLOCK * NA_HEAD_DIM
N_PAIRS = NA_HEADS // HEADS_PER_BLOCK
Q_TILE_ROWS = 4
Q_TILE = Q_TILE_ROWS * GRID_W
NA_TILE_GROUP = 1
NA_BATCH_BLOCK = 2
LOG2_E = float(np.log2(np.e))
Q_SCALE = NA_HEAD_DIM ** -0.5 * LOG2_E
N_DR0 = 2 * NA_WIN_ROWS - 2
DR_LO = NA_WIN_ROWS - 1 - NA_WIN_ROWS // 2
DR_HI = DR_LO + NA_WIN_ROWS - 1

GMLP_CHUNK = 128
GMLP_GROUPS = 8
POOL_WINDOWS = (2, 4, 8, 16)
POOL_GROUP = D_MODEL // len(POOL_WINDOWS)
POOL_HALO = 8
POOL_PAD = 16
POOL_PASS_BLOCKS = 2
POOL_CHUNK = 32

MOD_ROWS = 32
ADA_TN = 1024
TM = 1024
ROW_BLOCK = 256
POST_TM = 1024
VMEM_LIMIT_BYTES = 56 * 1024 * 1024

BF16 = jnp.bfloat16
F32 = jnp.float32


def _dot(a, b):
    return jnp.dot(a, b, preferred_element_type=F32)


def _dot_nt(a, b):
    return lax.dot_general(a, b, (((1,), (1,)), ((), ())), preferred_element_type=F32)


def _silu(x):
    return x / (1.0 + jnp.exp(-x))


def _const_spec(shape):
    nd = len(shape)
    return pl.BlockSpec(shape, lambda *_: (0,) * nd, pipeline_mode=pl.Buffered(1))


def _layer_spec(shape, layer):
    nd = len(shape)
    return pl.BlockSpec((1,) + tuple(shape[1:]), lambda *_: (layer,) + (0,) * (nd - 1),
                        pipeline_mode=pl.Buffered(1))


def _params(n_axes=1):
    return pltpu.CompilerParams(
        dimension_semantics=("arbitrary",) * n_axes, vmem_limit_bytes=VMEM_LIMIT_BYTES)


def _norm_mod(x, g, sc, sh):
    ms = jnp.mean(x * x, axis=-1, keepdims=True)
    return (x * lax.rsqrt(ms + EPS)) * (g * (1.0 + sc)) + sh


def _ada_kernel(c_ref, w_ref, b_ref, o_ref):
    s = _silu(c_ref[...]).astype(BF16)
    o_ref[0] = _dot(s, w_ref[0].astype(BF16)) + b_ref[0]


def _ada_mods(c_all, ada_w, ada_b):
    n = ada_w.shape[-1]
    return pl.pallas_call(
        _ada_kernel,
        grid=(DEPTH, n // ADA_TN),
        in_specs=[
            pl.BlockSpec((MOD_ROWS, D_MODEL), lambda i, j: (0, 0)),
            pl.BlockSpec((1, D_MODEL, ADA_TN), lambda i, j: (i, 0, j)),
            pl.BlockSpec((1, 1, ADA_TN), lambda i, j: (i, 0, j)),
        ],
        out_specs=pl.BlockSpec((1, MOD_ROWS, ADA_TN), lambda i, j: (i, 0, j)),
        out_shape=jax.ShapeDtypeStruct((DEPTH, MOD_ROWS, n), F32),
        compiler_params=_params(2),
        name="ada_mods",
    )(c_all, ada_w, ada_b.reshape(DEPTH, 1, n))


def _residual_ffn(mix, x_ref, mod, n2g_ref, wg_ref, wu_ref, wd_ref, o_ref, prepare=None, group=None):
    g1, sh2, sc2, g2 = mod[2:3], mod[3:4], mod[4:5], mod[5:6]
    starts = list(range(0, x_ref.shape[0], ROW_BLOCK))
    group = group or len(starts)
    passes = [starts[p:p + group] for p in range(0, len(starts), group)]
    if prepare is not None:
        for r0 in passes[0]:
            prepare(r0)
    for p, pass_starts in enumerate(passes):
        blocks = [pl.ds(r0, ROW_BLOCK) for r0 in pass_starts]
        ys = [mix(rows) for rows in blocks]
        x1s = [x_ref[rows, :] + g1 * y for rows, y in zip(blocks, ys)]
        h2s = [_norm_mod(x1, n2g_ref[...], sc2, sh2).astype(BF16) for x1 in x1s]
        gus = [(_dot(h2, wg_ref[0]), _dot(h2, wu_ref[0])) for h2 in h2s]
        if prepare is not None and p + 1 < len(passes):
            for r0 in passes[p + 1]:
                prepare(r0)
        acts = [(_silu(gate) * up).astype(BF16) for gate, up in gus]
        for rows, x1, act in zip(blocks, x1s, acts):
            o_ref[rows, :] = x1 + g2 * _dot(act, wd_ref[0])


def _post_kernel(a_ref, x_ref, mod_ref, n2g_ref, wm_ref, wg_ref, wu_ref, wd_ref, o_ref):
    mix = lambda rows: _dot(a_ref[rows, :], wm_ref[...])
    _residual_ffn(mix, x_ref, mod_ref[0], n2g_ref, wg_ref, wu_ref, wd_ref, o_ref)


def _post(a, x, mod, rows_per_mod, n2g, mix_w, layer, wg, wu, wd):
    t = x.shape[0]
    tok = lambda i: (i, 0)
    return pl.pallas_call(
        _post_kernel,
        grid=(t // POST_TM,),
        in_specs=[
            pl.BlockSpec((POST_TM, D_MODEL), tok),
            pl.BlockSpec((POST_TM, D_MODEL), tok),
            pl.BlockSpec((1, 6, D_MODEL), lambda i: (i * POST_TM // rows_per_mod, 0, 0)),
            _const_spec((1, D_MODEL)),
            _const_spec(mix_w.shape),
        ] + [_layer_spec(w.shape, layer) for w in (wg, wu, wd)],
        out_specs=pl.BlockSpec((POST_TM, D_MODEL), tok),
        out_shape=jax.ShapeDtypeStruct((t, D_MODEL), F32),
        compiler_params=_params(),
        name="post_dense",
    )(a, x, mod, n2g, mix_w, wg, wu, wd)


def _qkv_kernel(x_ref, mod_ref, n1g_ref, w_ref, qg_ref, kg_ref, seg_ref, q_ref, k_ref, v_ref):
    mod = mod_ref[0]
    seg = seg_ref[...]
    seg_w = seg.shape[0]

    def mean_sq(t):
        return jnp.concatenate(
            [_dot((t[:, c:c + seg_w] * t[:, c:c + seg_w]).astype(BF16), seg)
             for c in range(0, D_MODEL, seg_w)], axis=-1) * (1.0 / NA_HEAD_DIM)

    blocks = [pl.ds(r, ROW_BLOCK) for r in range(0, x_ref.shape[0], ROW_BLOCK)]
    hs = [_norm_mod(x_ref[rows, :], n1g_ref[...], mod[1:2], mod[0:1]).astype(BF16) for rows in blocks]
    qkvs = [_dot(h, w_ref[...]) for h in hs]
    for rows, qkv in zip(blocks, qkvs):
        v_ref[rows, :] = qkv[:, 2 * D_MODEL:].astype(BF16)
    qs = [qkv[:, :D_MODEL] for qkv in qkvs]
    ks = [qkv[:, D_MODEL:2 * D_MODEL] for qkv in qkvs]
    q_ms = [mean_sq(q) for q in qs]
    k_ms = [mean_sq(k) for k in ks]
    for rows, q, k, qm, km in zip(blocks, qs, ks, q_ms, k_ms):
        q_ref[rows, :] = (q * lax.rsqrt(qm + EPS) * qg_ref[...] * Q_SCALE).astype(BF16)
        k_ref[rows, :] = (k * lax.rsqrt(km + EPS) * kg_ref[...]).astype(BF16)


def _qkv(x, mod, rows_per_mod, n1g, w_qkv, qg, kg, seg):
    t = x.shape[0]
    tok = lambda i: (i, 0)
    out = jax.ShapeDtypeStruct((t, D_MODEL), BF16)
    return pl.pallas_call(
        _qkv_kernel,
        grid=(t // TM,),
        in_specs=[
            pl.BlockSpec((TM, D_MODEL), tok),
            pl.BlockSpec((1, 6, D_MODEL), lambda i: (i * TM // rows_per_mod, 0, 0)),
            _const_spec((1, D_MODEL)),
            _const_spec(w_qkv.shape),
            _const_spec((1, D_MODEL)),
            _const_spec((1, D_MODEL)),
            _const_spec(seg.shape),
        ],
        out_specs=[pl.BlockSpec((TM, D_MODEL), tok)] * 3,
        out_shape=[out, out, out],
        compiler_params=_params(),
        name="qkv",
    )(x, mod, n1g, w_qkv, qg, kg, seg)


def _softmax_rows(s_parts, biases):
    biased = [s if b is None else s + b for s, b in zip(s_parts, biases)]
    m = jnp.max(functools.reduce(jnp.maximum, biased), axis=-1, keepdims=True)
    p_parts = [jnp.exp2(s - m) for s in biased]
    l = jnp.sum(functools.reduce(jnp.add, p_parts), axis=-1, keepdims=True)
    return p_parts, l


def _stack_heads(qt, first):
    zero = jnp.zeros_like(qt)
    return jnp.concatenate([jnp.where(first, qt, zero), jnp.where(first, zero, qt)], axis=0)


def _na_kernel(with_ctx_out, q_ref, k_ref, v_ref, *rest):
    if with_ctx_out:
        qc_ref, kc_ref, vc_ref, ring_ref, mask_ref, o_ref, oc_ref, tab_ref = rest
    else:
        kc_ref, vc_ref, ring_ref, mask_ref, o_ref, tab_ref = rest
    first = lax.broadcasted_iota(jnp.int32, (1, PAIR_W), 1) < NA_HEAD_DIM

    @pl.when(pl.program_id(1) == 0)
    def _():
        for hh in range(HEADS_PER_BLOCK):
            def toeplitz(dr, shift):
                rows = jnp.broadcast_to(ring_ref[hh, dr:dr + 1, :], (GRID_W, PAIR_W))
                return pltpu.roll(rows, shift, 1, stride=1, stride_axis=0)
            pair = lambda dr0: jnp.where(first, toeplitz(dr0, 0), toeplitz(dr0 + 1, NA_HEAD_DIM))
            for dr0 in range(N_DR0):
                tab_ref[hh, dr0] = pair(dr0) * LOG2_E + mask_ref[0]
            tab_ref[hh, N_DR0] = pair(DR_LO - 1) * LOG2_E + mask_ref[1]
            tab_ref[hh, N_DR0 + 1] = pair(DR_HI) * LOG2_E + mask_ref[2]

    for bb in range(q_ref.shape[0]):
        _na_batch(first, tab_ref, q_ref.at[bb], k_ref.at[bb], v_ref.at[bb],
                  qc_ref.at[bb] if with_ctx_out else None, kc_ref.at[bb], vc_ref.at[bb],
                  o_ref.at[bb], oc_ref.at[bb] if with_ctx_out else None)


def _na_batch(first, tab_ref, q_ref, k_ref, v_ref, qc_ref, kc_ref, vc_ref, o_ref, oc_ref):
    seq = q_ref.shape[0]
    n_tiles = seq // Q_TILE
    kc = kc_ref[...]
    vc = vc_ref[...]
    ctx_parts = range(0, kc.shape[0], PAIR_W)

    def tab_index(dr0, clamped):
        if not clamped and dr0 == DR_LO - 1:
            return N_DR0
        if not clamped and dr0 == DR_HI:
            return N_DR0 + 1
        return dr0

    if qc_ref is not None:
        n = qc_ref.shape[0]
        (p,), l = _softmax_rows([_dot_nt(_stack_heads(qc_ref[...], first), kc)], [None])
        oc = _dot(p.astype(BF16), vc) * (1.0 / l)
        oc_ref[...] = jnp.where(first, oc[:n], oc[n:]).astype(oc_ref.dtype)

    def scores(q0, k0, win_rows, *_):
        qs = _stack_heads(q_ref[pl.ds(q0, Q_TILE), :], first)
        return _dot_nt(qs, k_ref[pl.ds(k0, win_rows * GRID_W), :]), _dot_nt(qs, kc)

    def probs(s_loc, s_ctx, q0, k0, win_rows, dr_base, clamped):
        p_loc, p_ctx, inv_l = [], [], []
        zero = jnp.zeros((GRID_W, PAIR_W), BF16)
        for hh in range(HEADS_PER_BLOCK):
            for i in range(Q_TILE_ROWS):
                r0 = hh * Q_TILE + i * GRID_W
                lo, hi = (0, win_rows) if clamped else (i, i + NA_WIN_ROWS)
                pairs = range(lo // 2, (hi + 1) // 2)
                parts = [s_loc[r0:r0 + GRID_W, wp * PAIR_W:(wp + 1) * PAIR_W] for wp in pairs]
                biases = [tab_ref[hh, tab_index(2 * wp - i + dr_base, clamped)] for wp in pairs]
                parts += [s_ctx[r0:r0 + GRID_W, c:c + PAIR_W] for c in ctx_parts]
                biases += [None] * len(ctx_parts)
                p_parts, l = _softmax_rows(parts, biases)
                p_loc.append(jnp.concatenate(
                    [p_parts[pairs.index(wp)].astype(BF16) if wp in pairs else zero
                     for wp in range(win_rows // 2)], axis=-1))
                p_ctx.append(jnp.concatenate([p.astype(BF16) for p in p_parts[len(pairs):]], axis=-1))
                inv_l.append(1.0 / l)
        return (jnp.concatenate(p_loc, axis=0), jnp.concatenate(p_ctx, axis=0),
                jnp.concatenate(inv_l, axis=0))

    def output(p_loc, p_ctx, inv_l, q0, k0, win_rows, *_):
        o = (_dot(p_loc, v_ref[pl.ds(k0, win_rows * GRID_W), :]) + _dot(p_ctx, vc)) * inv_l
        o_ref[pl.ds(q0, Q_TILE), :] = jnp.where(first, o[:Q_TILE], o[Q_TILE:]).astype(o_ref.dtype)

    tiles = [(0, 0, NA_WIN_ROWS, NA_WIN_ROWS - 1, True),
             (seq - Q_TILE, seq - NA_WIN_ROWS * GRID_W, NA_WIN_ROWS, Q_TILE_ROWS - 1, True)]
    tiles += [(t * Q_TILE, t * Q_TILE - (NA_WIN_ROWS // 2) * GRID_W, NA_WIN_ROWS + Q_TILE_ROWS, DR_LO, False)
              for t in range(1, n_tiles - 1)]
    for g0 in range(0, len(tiles), NA_TILE_GROUP):
        group = tiles[g0:g0 + NA_TILE_GROUP]
        ss = [scores(*cfg) for cfg in group]
        ps = [probs(*s, *cfg) for s, cfg in zip(ss, group)]
        for p, cfg in zip(ps, group):
            output(*p, *cfg)


def _na_attention(q, k, v, qc, kc, vc, ring):
    b, seq, _ = q.shape
    ctx_len = kc.shape[1]
    lat = pl.BlockSpec((NA_BATCH_BLOCK, seq, PAIR_W), lambda j, i: (i, 0, j))
    ctx = pl.BlockSpec((NA_BATCH_BLOCK, ctx_len, PAIR_W), lambda j, i: (i, 0, j))
    ring_spec = pl.BlockSpec((HEADS_PER_BLOCK,) + ring.shape[1:], lambda j, i: (j, 0, 0))
    masks = _bias_masks()
    with_ctx_out = qc is not None
    lat_shape = jax.ShapeDtypeStruct((b, seq, D_MODEL), BF16)
    ctx_shape = jax.ShapeDtypeStruct((b, ctx_len, D_MODEL), BF16)
    return pl.pallas_call(
        functools.partial(_na_kernel, with_ctx_out),
        grid=(N_PAIRS, b // NA_BATCH_BLOCK),
        in_specs=[lat, lat, lat] + [ctx] * (3 if with_ctx_out else 2) + [ring_spec, _const_spec(masks.shape)],
        out_specs=[lat, ctx] if with_ctx_out else lat,
        out_shape=[lat_shape, ctx_shape] if with_ctx_out else lat_shape,
        scratch_shapes=[pltpu.VMEM((HEADS_PER_BLOCK, N_DR0 + 2, GRID_W, PAIR_W), F32)],
        compiler_params=_params(2),
        name="na_attention",
    )(q, k, v, *([qc] if with_ctx_out else []), kc, vc, ring, masks)


def _bias_ring(rpb):
    n_dc = 2 * NA_WIN_COLS - 1
    return jnp.concatenate([rpb[..., NA_WIN_COLS - 1:],
                            jnp.zeros(rpb.shape[:2] + (PAIR_W - n_dc,), rpb.dtype),
                            rpb[..., :NA_WIN_COLS - 1]], axis=-1).astype(F32)


def _bias_masks():
    qcol = np.arange(GRID_W)[:, None]
    kcol = np.arange(GRID_W)[None, :]
    cstart = np.clip(qcol - NA_WIN_COLS // 2, 0, GRID_W - NA_WIN_COLS)
    col_ok = np.tile((kcol >= cstart) & (kcol < cstart + NA_WIN_COLS), (1, 2))
    second = np.arange(PAIR_W)[None, :] >= GRID_W
    oks = [col_ok, col_ok & second, col_ok & ~second]
    return jnp.asarray(np.stack([np.where(ok, 0.0, NEG_INF) for ok in oks]), F32)


def _gelu_tanh(x):
    c = float(np.sqrt(2.0 / np.pi))
    return x * (0.5 + 0.5 * jnp.tanh(x * (c + (c * 0.044715) * (x * x))))


def _gmlp_kernel(x_ref, mod_ref, n1g_ref, win_ref, bin_ref, lng_ref, lnb_ref, ws_ref, bs_ref, a_ref):
    mod = mod_ref[0]
    gw = D_MODEL // GMLP_GROUPS

    def layer_norm(v):
        mu = jnp.mean(v, axis=-1, keepdims=True)
        vc = v - mu
        var = jnp.mean(vc * vc, axis=-1, keepdims=True)
        return (vc * lax.rsqrt(var + EPS) * lng_ref[...] + lnb_ref[...]).astype(BF16)

    rows_per_block = ROW_BLOCK
    blocks = list(range(0, x_ref.shape[0], ROW_BLOCK))
    hs = [_norm_mod(x_ref[pl.ds(r0, rows_per_block), :], n1g_ref[...], mod[1:2], mod[0:1]).astype(BF16)
          for r0 in blocks]
    zs = [_dot(h, win_ref[...]) for h in hs]
    zs = [_gelu_tanh(z + bin_ref[...]) for z in zs]
    vns = [layer_norm(z[:, D_MODEL:]) for z in zs]
    chunks = [slice(n * GMLP_CHUNK, (n + 1) * GMLP_CHUNK) for n in range(rows_per_block // GMLP_CHUNK)]
    for r0, z, vn in zip(blocks, zs, vns):
        for g in range(GMLP_GROUPS):
            cols = slice(g * gw, (g + 1) * gw)
            mixed = _dot(ws_ref[g], jnp.concatenate([vn[rows, cols] for rows in chunks], axis=1))
            for n, rows in enumerate(chunks):
                a_ref[pl.ds(r0 + rows.start, GMLP_CHUNK), cols] = (
                    z[rows, cols] * (mixed[:, n * gw:(n + 1) * gw] + bs_ref[g])).astype(a_ref.dtype)


def _gmlp(x, mod, rows_per_mod, n1g, w_in, b_in, ln_g, ln_b, w_s, b_s):
    t = x.shape[0]
    tok = lambda i: (i, 0)
    return pl.pallas_call(
        _gmlp_kernel,
        grid=(t // TM,),
        in_specs=[
            pl.BlockSpec((TM, D_MODEL), tok),
            pl.BlockSpec((1, 6, D_MODEL), lambda i: (i * TM // rows_per_mod, 0, 0)),
            _const_spec((1, D_MODEL)),
            _const_spec(w_in.shape),
            _const_spec(b_in.shape),
            _const_spec((1, D_MODEL)),
            _const_spec((1, D_MODEL)),
            _const_spec(w_s.shape),
            _const_spec(b_s.shape),
        ],
        out_specs=pl.BlockSpec((TM, D_MODEL), tok),
        out_shape=jax.ShapeDtypeStruct((t, D_MODEL), BF16),
        compiler_params=_params(),
        name="gmlp",
    )(x, mod, n1g, w_in, b_in, ln_g, ln_b, w_s, b_s)


def _pool_post_kernel(seq_len, x_ref, xp_ref, xn_ref, mod_ref, n1g_ref, n2g_ref, wp_ref, ps_ref,
                      wg_ref, wu_ref, wd_ref, o_ref, hbuf, sbuf, abuf):
    tm = x_ref.shape[0]
    tiles_per_seq = seq_len // tm
    pos = pl.program_id(0) % tiles_per_seq
    mod = mod_ref[0]
    n1g = n1g_ref[...]
    norm = lambda x: _norm_mod(x, n1g, mod[1:2], mod[0:1])
    zero = jnp.zeros((POOL_HALO, D_MODEL), F32)
    h0 = POOL_HALO
    chunks = lambda n: [(r, min(POOL_CHUNK, n - r)) for r in range(0, n, POOL_CHUNK)]
    hbuf[0:h0] = jnp.where(pos > 0, norm(xp_ref[...]), zero)
    for r, n in chunks(tm):
        hbuf[h0 + r:h0 + r + n] = norm(x_ref[r:r + n])
    hbuf[h0 + tm:h0 + tm + POOL_HALO] = jnp.where(pos < tiles_per_seq - 1, norm(xn_ref[...]), zero)
    hbuf[h0 + tm + POOL_HALO:] = jnp.zeros((POOL_PAD, D_MODEL), F32)

    def pooled_rows(r0):
        for g, w in enumerate(POOL_WINDOWS):
            cols = slice(g * POOL_GROUP, (g + 1) * POOL_GROUP)
            n_levels = g
            src = lambda lo, hi: hbuf[r0 + lo:r0 + hi, cols]
            for k in range(n_levels):
                step = 2 ** k
                for r, n in chunks(ROW_BLOCK + POOL_HALO * (n_levels - k)):
                    sbuf[k, r:r + n] = src(r, r + n) + src(r + step, r + n + step)
                src = functools.partial(lambda k, lo, hi: sbuf[k, lo:hi], k)
            span = 2 ** n_levels
            lo = h0 - w // 2
            for r, n in chunks(ROW_BLOCK):
                total = functools.reduce(
                    jnp.add, [src(lo + r + j, lo + r + j + n) for j in range(0, w, span)])
                t = pos * tm + r0 + r + lax.broadcasted_iota(jnp.int32, (n, 1), 0)
                cnt = jnp.minimum(t - w // 2 + w, seq_len) - jnp.maximum(t - w // 2, 0)
                mean = total * (1.0 / cnt.astype(F32))
                rows = slice(r0 + r, r0 + r + n)
                abuf[rows, cols] = (mean - hbuf[h0 + r0 + r:h0 + r0 + r + n, cols]).astype(abuf.dtype)

    def mix(rows):
        a = abuf[rows, :]
        return jnp.concatenate(
            [_dot(a[:, g * POOL_GROUP:(g + 1) * POOL_GROUP], wp_ref[g])
             for g in range(len(POOL_WINDOWS))], axis=-1) * ps_ref[...]

    _residual_ffn(mix, x_ref, mod, n2g_ref, wg_ref, wu_ref, wd_ref, o_ref,
                  prepare=pooled_rows, group=POOL_PASS_BLOCKS)


def _pool_post(x, seq_len, tm, mod, rows_per_mod, n1g, n2g, w_pool, pool_scale, layer, wg, wu, wd):
    t = x.shape[0]
    hb = tm // POOL_HALO
    n_halo = t // POOL_HALO
    n_levels = len(POOL_WINDOWS) - 1
    return pl.pallas_call(
        functools.partial(_pool_post_kernel, seq_len),
        grid=(t // tm,),
        in_specs=[
            pl.BlockSpec((tm, D_MODEL), lambda i: (i, 0)),
            pl.BlockSpec((POOL_HALO, D_MODEL), lambda i: (jnp.maximum(i * hb - 1, 0), 0)),
            pl.BlockSpec((POOL_HALO, D_MODEL), lambda i: (jnp.minimum((i + 1) * hb, n_halo - 1), 0)),
            pl.BlockSpec((1, 6, D_MODEL), lambda i: (i * tm // rows_per_mod, 0, 0)),
            _const_spec((1, D_MODEL)),
            _const_spec((1, D_MODEL)),
            _const_spec(w_pool.shape),
            _const_spec((1, D_MODEL)),
        ] + [_layer_spec(w.shape, layer) for w in (wg, wu, wd)],
        out_specs=pl.BlockSpec((tm, D_MODEL), lambda i: (i, 0)),
        out_shape=jax.ShapeDtypeStruct((t, D_MODEL), F32),
        scratch_shapes=[
            pltpu.VMEM((tm + 2 * POOL_HALO + POOL_PAD, D_MODEL), F32),
            pltpu.VMEM((n_levels, ROW_BLOCK + POOL_HALO * n_levels, POOL_GROUP), F32),
            pltpu.VMEM((tm, D_MODEL), BF16),
        ],
        compiler_params=_params(),
        name="pool_post",
    )(x, x, x, mod, n1g, n2g, w_pool, pool_scale, wg, wu, wd)


def kernel(x, c, ctx, c_ctx, ada_w, ada_b, norm1_g, norm2_g, ffn_w_gate, ffn_w_up, ffn_w_down, na_w_qkv, na_w_o, na_q_norm, na_k_norm, na_rpb, gm_w_in, gm_b_in, gm_ln_g, gm_ln_b, gm_w_s, gm_b_s, gm_w_out, pool_w, pool_scale):
    b, seq, d = x.shape
    ctx_len = ctx.shape[1]
    assert d == D_MODEL and seq % POST_TM == 0 and (b * ctx_len) % POST_TM == 0 and seq % (Q_TILE * 2) == 0

    c_all = jnp.zeros((MOD_ROWS, d), F32).at[:b].set(c).at[b].set(c_ctx)
    mods = _ada_mods(c_all, ada_w, ada_b)

    heads_per_seg = 4
    seg = jnp.asarray(np.kron(np.eye(heads_per_seg), np.ones((NA_HEAD_DIM, NA_HEAD_DIM))), BF16)
    row = lambda v: v.reshape(1, d)
    tile_gain = lambda g: jnp.tile(g, NA_HEADS).reshape(1, d)

    wg, wu, wd = (w.astype(BF16) for w in (ffn_w_gate, ffn_w_up, ffn_w_down))
    xl = x.reshape(b * seq, d)
    xc = ctx.reshape(b * ctx_len, d)
    for i in range(DEPTH):
        kind, j = i % N_MIXERS, i // N_MIXERS
        last = i == DEPTH - 1
        need_ctx = (not last) or kind == 0
        mod_l = mods[i, :b].reshape(b, 6, d)
        mod_c = mods[i, b:b + 1].reshape(1, 6, d)
        streams = [(xl, mod_l, seq, seq)]
        if need_ctx:
            streams.append((xc, mod_c, b * ctx_len, ctx_len))
        n1g, n2g = row(norm1_g[i]), row(norm2_g[i])

        if kind == 2:
            outs = [_pool_post(xs, sl, min(sl, POST_TM), m, rpm, n1g, n2g, pool_w[j].astype(BF16),
                               row(pool_scale[j]), i, wg, wu, wd) for xs, m, rpm, sl in streams]
        elif kind == 0:
            w_qkv = na_w_qkv[j].astype(BF16)
            qg, kg = tile_gain(na_q_norm[j]), tile_gain(na_k_norm[j])
            (ql, kl, vl), (qc, kc, vc) = (
                tuple(t.reshape(b, -1, d) for t in _qkv(xs, m, rpm, n1g, w_qkv, qg, kg, seg))
                for xs, m, rpm, _ in streams)
            acts = _na_attention(ql, kl, vl, None if last else qc, kc, vc, _bias_ring(na_rpb[j]))
            acts = [a.reshape(-1, d) for a in (acts if isinstance(acts, (list, tuple)) else [acts])]
            mix_w = na_w_o[j].astype(BF16)
        else:
            w_in = gm_w_in[j].astype(BF16)
            b_in = gm_b_in[j].reshape(1, -1)
            w_s = gm_w_s[j].astype(BF16)
            b_s = jnp.broadcast_to(gm_b_s[j][:, :, None], (GMLP_GROUPS, GMLP_CHUNK, d // GMLP_GROUPS))
            acts = [_gmlp(xs, m, rpm, n1g, w_in, b_in, row(gm_ln_g[j]), row(gm_ln_b[j]), w_s, b_s)
                    for xs, m, rpm, _ in streams]
            mix_w = gm_w_out[j].astype(BF16)

        if kind != 2:
            outs = [_post(a, xs, m, rpm, n2g, mix_w, i, wg, wu, wd)
                    for a, (xs, m, rpm, _) in zip(acts, streams)]
        xl = outs[0]
        if not last:
            xc = outs[1]
    return xl.reshape(b, seq, d)
```

```python
import functools

import numpy as np
import jax
import jax.numpy as jnp
from jax import lax
from jax.experimental import pallas as pl
from jax.experimental.pallas import tpu as pltpu

D_MODEL = 1024
DEPTH = 4
GRID_W = 64
N_MIXERS = 3
EPS = 1e-6
NEG_INF = -1e30

NA_HEADS = 16
NA_HEAD_DIM = D_MODEL // NA_HEADS
NA_WIN_ROWS = 8
NA_WIN_COLS = 16
HEADS_PER_BLOCK = 2
PAIR_W = HEADS_PER_BLOCK * NA_HEAD_DIM
N_PAIRS = NA_HEADS // HEADS_PER_BLOCK
Q_TILE_ROWS = 4
Q_TILE = Q_TILE_ROWS * GRID_W
NA_TILE_GROUP = 1
CTX_OUT_AT = 2
NA_BATCH_BLOCK = 2
LOG2_E = float(np.log2(np.e))
Q_SCALE = NA_HEAD_DIM ** -0.5 * LOG2_E
N_DR0 = 2 * NA_WIN_ROWS - 2
DR_LO = NA_WIN_ROWS - 1 - NA_WIN_ROWS // 2
DR_HI = DR_LO + NA_WIN_ROWS - 1

GMLP_CHUNK = 128
GMLP_GROUPS = 8
POOL_WINDOWS = (2, 4, 8, 16)
POOL_GROUP = D_MODEL // len(POOL_WINDOWS)
POOL_HALO = 8
POOL_PAD = 16
POOL_PASS_BLOCKS = 2
POOL_CHUNK = 32

MOD_ROWS = 32
ADA_TN = 1024
TM = 1024
ROW_BLOCK = 256
POST_TM = 1024
VMEM_LIMIT_BYTES = 56 * 1024 * 1024

BF16 = jnp.bfloat16
F32 = jnp.float32


def _dot(a, b):
    return jnp.dot(a, b, preferred_element_type=F32)


def _dot_nt(a, b):
    return lax.dot_general(a, b, (((1,), (1,)), ((), ())), preferred_element_type=F32)


def _silu(x):
    return x / (1.0 + jnp.exp(-x))


def _const_spec(shape):
    nd = len(shape)
    return pl.BlockSpec(shape, lambda *_: (0,) * nd, pipeline_mode=pl.Buffered(1))


def _layer_spec(shape, layer):
    nd = len(shape)
    return pl.BlockSpec((1,) + tuple(shape[1:]), lambda *_: (layer,) + (0,) * (nd - 1),
                        pipeline_mode=pl.Buffered(1))


def _params(n_axes=1):
    return pltpu.CompilerParams(
        dimension_semantics=("arbitrary",) * n_axes, vmem_limit_bytes=VMEM_LIMIT_BYTES)


def _norm_mod(x, g, sc, sh):
    ms = jnp.mean(x * x, axis=-1, keepdims=True)
    return (x * lax.rsqrt(ms + EPS)) * (g * (1.0 + sc)) + sh


def _ada_kernel(c_ref, w_ref, b_ref, o_ref):
    s = _silu(c_ref[...]).astype(BF16)
    o_ref[0] = _dot(s, w_ref[0].astype(BF16)) + b_ref[0]


def _ada_mods(c_all, ada_w, ada_b):
    n = ada_w.shape[-1]
    return pl.pallas_call(
        _ada_kernel,
        grid=(DEPTH, n // ADA_TN),
        in_specs=[
            pl.BlockSpec((MOD_ROWS, D_MODEL), lambda i, j: (0, 0)),
            pl.BlockSpec((1, D_MODEL, ADA_TN), lambda i, j: (i, 0, j)),
            pl.BlockSpec((1, 1, ADA_TN), lambda i, j: (i, 0, j)),
        ],
        out_specs=pl.BlockSpec((1, MOD_ROWS, ADA_TN), lambda i, j: (i, 0, j)),
        out_shape=jax.ShapeDtypeStruct((DEPTH, MOD_ROWS, n), F32),
        compiler_params=_params(2),
        name="ada_mods",
    )(c_all, ada_w, ada_b.reshape(DEPTH, 1, n))


def _residual_ffn(mix, x_ref, mod, n2g_ref, wg_ref, wu_ref, wd_ref, o_ref, prepare=None, group=None):
    g1, sh2, sc2, g2 = mod[2:3], mod[3:4], mod[4:5], mod[5:6]
    starts = list(range(0, x_ref.shape[0], ROW_BLOCK))
    group = group or len(starts)
    passes = [starts[p:p + group] for p in range(0, len(starts), group)]
    if prepare is not None:
        for r0 in passes[0]:
            prepare(r0)
    for p, pass_starts in enumerate(passes):
        blocks = [pl.ds(r0, ROW_BLOCK) for r0 in pass_starts]
        ys = [mix(rows) for rows in blocks]
        x1s = [x_ref[rows, :] + g1 * y for rows, y in zip(blocks, ys)]
        h2s = [_norm_mod(x1, n2g_ref[...], sc2, sh2).astype(BF16) for x1 in x1s]
        gus = [(_dot(h2, wg_ref[0]), _dot(h2, wu_ref[0])) for h2 in h2s]
        if prepare is not None and p + 1 < len(passes):
            for r0 in passes[p + 1]:
                prepare(r0)
        acts = [(_silu(gate) * up).astype(BF16) for gate, up in gus]
        for rows, x1, act in zip(blocks, x1s, acts):
            o_ref[rows, :] = x1 + g2 * _dot(act, wd_ref[0])


def _post_kernel(a_ref, x_ref, mod_ref, n2g_ref, wm_ref, wg_ref, wu_ref, wd_ref, o_ref):
    mix = lambda rows: _dot(a_ref[rows, :], wm_ref[...])
    _residual_ffn(mix, x_ref, mod_ref[0], n2g_ref, wg_ref, wu_ref, wd_ref, o_ref)


def _post(a, x, mod, rows_per_mod, n2g, mix_w, layer, wg, wu, wd):
    t = x.shape[0]
    tok = lambda i: (i, 0)
    return pl.pallas_call(
        _post_kernel,
        grid=(t // POST_TM,),
        in_specs=[
            pl.BlockSpec((POST_TM, D_MODEL), tok),
            pl.BlockSpec((POST_TM, D_MODEL), tok),
            pl.BlockSpec((1, 6, D_MODEL), lambda i: (i * POST_TM // rows_per_mod, 0, 0)),
            _const_spec((1, D_MODEL)),
            _const_spec(mix_w.shape),
        ] + [_layer_spec(w.shape, layer) for w in (wg, wu, wd)],
        out_specs=pl.BlockSpec((POST_TM, D_MODEL), tok),
        out_shape=jax.ShapeDtypeStruct((t, D_MODEL), F32),
        compiler_params=_params(),
        name="post_dense",
    )(a, x, mod, n2g, mix_w, wg, wu, wd)


def _qkv_kernel(x_ref, mod_ref, n1g_ref, w_ref, qg_ref, kg_ref, seg_ref, q_ref, k_ref, v_ref):
    mod = mod_ref[0]
    seg = seg_ref[...]
    seg_w = seg.shape[0]

    def mean_sq(t):
        return jnp.concatenate(
            [_dot((t[:, c:c + seg_w] * t[:, c:c + seg_w]).astype(BF16), seg)
             for c in range(0, D_MODEL, seg_w)], axis=-1) * (1.0 / NA_HEAD_DIM)

    blocks = [pl.ds(r, ROW_BLOCK) for r in range(0, x_ref.shape[0], ROW_BLOCK)]
    hs = [_norm_mod(x_ref[rows, :], n1g_ref[...], mod[1:2], mod[0:1]).astype(BF16) for rows in blocks]
    qkvs = [_dot(h, w_ref[...]) for h in hs]
    for rows, qkv in zip(blocks, qkvs):
        v_ref[rows, :] = qkv[:, 2 * D_MODEL:].astype(BF16)
    qs = [qkv[:, :D_MODEL] for qkv in qkvs]
    ks = [qkv[:, D_MODEL:2 * D_MODEL] for qkv in qkvs]
    q_ms = [mean_sq(q) for q in qs]
    k_ms = [mean_sq(k) for k in ks]
    for rows, q, k, qm, km in zip(blocks, qs, ks, q_ms, k_ms):
        q_ref[rows, :] = (q * lax.rsqrt(qm + EPS) * qg_ref[...] * Q_SCALE).astype(BF16)
        k_ref[rows, :] = (k * lax.rsqrt(km + EPS) * kg_ref[...]).astype(BF16)


def _qkv(x, mod, rows_per_mod, n1g, w_qkv, qg, kg, seg):
    t = x.shape[0]
    tok = lambda i: (i, 0)
    out = jax.ShapeDtypeStruct((t, D_MODEL), BF16)
    return pl.pallas_call(
        _qkv_kernel,
        grid=(t // TM,),
        in_specs=[
            pl.BlockSpec((TM, D_MODEL), tok),
            pl.BlockSpec((1, 6, D_MODEL), lambda i: (i * TM // rows_per_mod, 0, 0)),
            _const_spec((1, D_MODEL)),
            _const_spec(w_qkv.shape),
            _const_spec((1, D_MODEL)),
            _const_spec((1, D_MODEL)),
            _const_spec(seg.shape),
        ],
        out_specs=[pl.BlockSpec((TM, D_MODEL), tok)] * 3,
        out_shape=[out, out, out],
        compiler_params=_params(),
        name="qkv",
    )(x, mod, n1g, w_qkv, qg, kg, seg)


def _softmax_rows(s_parts, biases):
    biased = [s if b is None else s + b for s, b in zip(s_parts, biases)]
    m = jnp.max(functools.reduce(jnp.maximum, biased), axis=-1, keepdims=True)
    p_parts = [jnp.exp2(s - m) for s in biased]
    l = jnp.sum(functools.reduce(jnp.add, p_parts), axis=-1, keepdims=True)
    return p_parts, l


def _stack_heads(qt, first):
    zero = jnp.zeros_like(qt)
    return jnp.concatenate([jnp.where(first, qt, zero), jnp.where(first, zero, qt)], axis=0)


def _na_kernel(with_ctx_out, q_ref, k_ref, v_ref, *rest):
    if with_ctx_out:
        qc_ref, kc_ref, vc_ref, ring_ref, mask_ref, o_ref, oc_ref, tab_ref = rest
    else:
        kc_ref, vc_ref, ring_ref, mask_ref, o_ref, tab_ref = rest
    first = lax.broadcasted_iota(jnp.int32, (1, PAIR_W), 1) < NA_HEAD_DIM

    @pl.when(pl.program_id(1) == 0)
    def _():
        for hh in range(HEADS_PER_BLOCK):
            def toeplitz(dr, shift):
                rows = jnp.broadcast_to(ring_ref[hh, dr:dr + 1, :], (GRID_W, PAIR_W))
                return pltpu.roll(rows, shift, 1, stride=1, stride_axis=0)
            pair = lambda dr0: jnp.where(first, toeplitz(dr0, 0), toeplitz(dr0 + 1, NA_HEAD_DIM))
            for dr0 in range(N_DR0):
                tab_ref[hh, dr0] = pair(dr0) * LOG2_E + mask_ref[0]
            tab_ref[hh, N_DR0] = pair(DR_LO - 1) * LOG2_E + mask_ref[1]
            tab_ref[hh, N_DR0 + 1] = pair(DR_HI) * LOG2_E + mask_ref[2]

    for bb in range(q_ref.shape[0]):
        _na_batch(first, tab_ref, q_ref.at[bb], k_ref.at[bb], v_ref.at[bb],
                  qc_ref.at[bb] if with_ctx_out else None, kc_ref.at[bb], vc_ref.at[bb],
                  o_ref.at[bb], oc_ref.at[bb] if with_ctx_out else None)


def _na_batch(first, tab_ref, q_ref, k_ref, v_ref, qc_ref, kc_ref, vc_ref, o_ref, oc_ref):
    seq = q_ref.shape[0]
    n_tiles = seq // Q_TILE
    kc = kc_ref[...]
    vc = vc_ref[...]
    ctx_parts = range(0, kc.shape[0], PAIR_W)

    def tab_index(dr0, clamped):
        if not clamped and dr0 == DR_LO - 1:
            return N_DR0
        if not clamped and dr0 == DR_HI:
            return N_DR0 + 1
        return dr0

    def context_output():
        n = qc_ref.shape[0]
        (p,), l = _softmax_rows([_dot_nt(_stack_heads(qc_ref[...], first), kc)], [None])
        oc = _dot(p.astype(BF16), vc) * (1.0 / l)
        oc_ref[...] = jnp.where(first, oc[:n], oc[n:]).astype(oc_ref.dtype)

    def scores(q0, k0, win_rows, *_):
        qs = _stack_heads(q_ref[pl.ds(q0, Q_TILE), :], first)
        return _dot_nt(qs, k_ref[pl.ds(k0, win_rows * GRID_W), :]), _dot_nt(qs, kc)

    def probs(s_loc, s_ctx, q0, k0, win_rows, dr_base, clamped):
        p_loc, p_ctx, inv_l = [], [], []
        zero = jnp.zeros((GRID_W, PAIR_W), BF16)
        for hh in range(HEADS_PER_BLOCK):
            for i in range(Q_TILE_ROWS):
                r0 = hh * Q_TILE + i * GRID_W
                lo, hi = (0, win_rows) if clamped else (i, i + NA_WIN_ROWS)
                pairs = range(lo // 2, (hi + 1) // 2)
                parts = [s_loc[r0:r0 + GRID_W, wp * PAIR_W:(wp + 1) * PAIR_W] for wp in pairs]
                biases = [tab_ref[hh, tab_index(2 * wp - i + dr_base, clamped)] for wp in pairs]
                parts += [s_ctx[r0:r0 + GRID_W, c:c + PAIR_W] for c in ctx_parts]
                biases += [None] * len(ctx_parts)
                p_parts, l = _softmax_rows(parts, biases)
                p_loc.append(jnp.concatenate(
                    [p_parts[pairs.index(wp)].astype(BF16) if wp in pairs else zero
                     for wp in range(win_rows // 2)], axis=-1))
                p_ctx.append(jnp.concatenate([p.astype(BF16) for p in p_parts[len(pairs):]], axis=-1))
                inv_l.append(1.0 / l)
        return (jnp.concatenate(p_loc, axis=0), jnp.concatenate(p_ctx, axis=0),
                jnp.concatenate(inv_l, axis=0))

    def output(p_loc, p_ctx, inv_l, q0, k0, win_rows, *_):
        o = (_dot(p_loc, v_ref[pl.ds(k0, win_rows * GRID_W), :]) + _dot(p_ctx, vc)) * inv_l
        o_ref[pl.ds(q0, Q_TILE), :] = jnp.where(first, o[:Q_TILE], o[Q_TILE:]).astype(o_ref.dtype)

    tiles = [(0, 0, NA_WIN_ROWS, NA_WIN_ROWS - 1, True),
             (seq - Q_TILE, seq - NA_WIN_ROWS * GRID_W, NA_WIN_ROWS, Q_TILE_ROWS - 1, True)]
    tiles += [(t * Q_TILE, t * Q_TILE - (NA_WIN_ROWS // 2) * GRID_W, NA_WIN_ROWS + Q_TILE_ROWS, DR_LO, False)
              for t in range(1, n_tiles - 1)]
    for g0 in range(0, len(tiles), NA_TILE_GROUP):
        if qc_ref is not None and g0 == CTX_OUT_AT:
            context_output()
        group = tiles[g0:g0 + NA_TILE_GROUP]
        ss = [scores(*cfg) for cfg in group]
        ps = [probs(*s, *cfg) for s, cfg in zip(ss, group)]
        for p, cfg in zip(ps, group):
            output(*p, *cfg)


def _na_attention(q, k, v, qc, kc, vc, ring):
    b, seq, _ = q.shape
    ctx_len = kc.shape[1]
    lat = pl.BlockSpec((NA_BATCH_BLOCK, seq, PAIR_W), lambda j, i: (i, 0, j))
    ctx = pl.BlockSpec((NA_BATCH_BLOCK, ctx_len, PAIR_W), lambda j, i: (i, 0, j))
    ring_spec = pl.BlockSpec((HEADS_PER_BLOCK,) + ring.shape[1:], lambda j, i: (j, 0, 0))
    masks = _bias_masks()
    with_ctx_out = qc is not None
    lat_shape = jax.ShapeDtypeStruct((b, seq, D_MODEL), BF16)
    ctx_shape = jax.ShapeDtypeStruct((b, ctx_len, D_MODEL), BF16)
    return pl.pallas_call(
        functools.partial(_na_kernel, with_ctx_out),
        grid=(N_PAIRS, b // NA_BATCH_BLOCK),
        in_specs=[lat, lat, lat] + [ctx] * (3 if with_ctx_out else 2) + [ring_spec, _const_spec(masks.shape)],
        out_specs=[lat, ctx] if with_ctx_out else lat,
        out_shape=[lat_shape, ctx_shape] if with_ctx_out else lat_shape,
        scratch_shapes=[pltpu.VMEM((HEADS_PER_BLOCK, N_DR0 + 2, GRID_W, PAIR_W), F32)],
        compiler_params=_params(2),
        name="na_attention",
    )(q, k, v, *([qc] if with_ctx_out else []), kc, vc, ring, masks)


def _bias_ring(rpb):
    n_dc = 2 * NA_WIN_COLS - 1
    return jnp.concatenate([rpb[..., NA_WIN_COLS - 1:],
                            jnp.zeros(rpb.shape[:2] + (PAIR_W - n_dc,), rpb.dtype),
                            rpb[..., :NA_WIN_COLS - 1]], axis=-1).astype(F32)


def _bias_masks():
    qcol = np.arange(GRID_W)[:, None]
    kcol = np.arange(GRID_W)[None, :]
    cstart = np.clip(qcol - NA_WIN_COLS // 2, 0, GRID_W - NA_WIN_COLS)
    col_ok = np.tile((kcol >= cstart) & (kcol < cstart + NA_WIN_COLS), (1, 2))
    second = np.arange(PAIR_W)[None, :] >= GRID_W
    oks = [col_ok, col_ok & second, col_ok & ~second]
    return jnp.asarray(np.stack([np.where(ok, 0.0, NEG_INF) for ok in oks]), F32)


def _gelu_tanh(x):
    c = float(np.sqrt(2.0 / np.pi))
    return x * (0.5 + 0.5 * jnp.tanh(x * (c + (c * 0.044715) * (x * x))))


def _gmlp_kernel(x_ref, mod_ref, n1g_ref, win_ref, bin_ref, lng_ref, lnb_ref, ws_ref, bs_ref, a_ref):
    mod = mod_ref[0]
    gw = D_MODEL // GMLP_GROUPS

    def layer_norm(v):
        mu = jnp.mean(v, axis=-1, keepdims=True)
        vc = v - mu
        var = jnp.mean(vc * vc, axis=-1, keepdims=True)
        return (vc * lax.rsqrt(var + EPS) * lng_ref[...] + lnb_ref[...]).astype(BF16)

    rows_per_block = ROW_BLOCK
    blocks = list(range(0, x_ref.shape[0], ROW_BLOCK))
    hs = [_norm_mod(x_ref[pl.ds(r0, rows_per_block), :], n1g_ref[...], mod[1:2], mod[0:1]).astype(BF16)
          for r0 in blocks]
    zs = [_dot(h, win_ref[...]) for h in hs]
    zs = [_gelu_tanh(z + bin_ref[...]) for z in zs]
    vns = [layer_norm(z[:, D_MODEL:]) for z in zs]
    chunks = [slice(n * GMLP_CHUNK, (n + 1) * GMLP_CHUNK) for n in range(rows_per_block // GMLP_CHUNK)]
    for r0, z, vn in zip(blocks, zs, vns):
        for g in range(GMLP_GROUPS):
            cols = slice(g * gw, (g + 1) * gw)
            mixed = _dot(ws_ref[g], jnp.concatenate([vn[rows, cols] for rows in chunks], axis=1))
            for n, rows in enumerate(chunks):
                a_ref[pl.ds(r0 + rows.start, GMLP_CHUNK), cols] = (
                    z[rows, cols] * (mixed[:, n * gw:(n + 1) * gw] + bs_ref[g])).astype(a_ref.dtype)


def _gmlp(x, mod, rows_per_mod, n1g, w_in, b_in, ln_g, ln_b, w_s, b_s):
    t = x.shape[0]
    tok = lambda i: (i, 0)
    return pl.pallas_call(
        _gmlp_kernel,
        grid=(t // TM,),
        in_specs=[
            pl.BlockSpec((TM, D_MODEL), tok),
            pl.BlockSpec((1, 6, D_MODEL), lambda i: (i * TM // rows_per_mod, 0, 0)),
            _const_spec((1, D_MODEL)),
            _const_spec(w_in.shape),
            _const_spec(b_in.shape),
            _const_spec((1, D_MODEL)),
            _const_spec((1, D_MODEL)),
            _const_spec(w_s.shape),
            _const_spec(b_s.shape),
        ],
        out_specs=pl.BlockSpec((TM, D_MODEL), tok),
        out_shape=jax.ShapeDtypeStruct((t, D_MODEL), BF16),
        compiler_params=_params(),
        name="gmlp",
    )(x, mod, n1g, w_in, b_in, ln_g, ln_b, w_s, b_s)


def _pool_post_kernel(seq_len, x_ref, xp_ref, xn_ref, mod_ref, n1g_ref, n2g_ref, wp_ref, ps_ref,
                      wg_ref, wu_ref, wd_ref, o_ref, hbuf, sbuf, abuf):
    tm = x_ref.shape[0]
    tiles_per_seq = seq_len // tm
    pos = pl.program_id(0) % tiles_per_seq
    mod = mod_ref[0]
    n1g = n1g_ref[...]
    norm = lambda x: _norm_mod(x, n1g, mod[1:2], mod[0:1])
    zero = jnp.zeros((POOL_HALO, D_MODEL), F32)
    h0 = POOL_HALO
    chunks = lambda n: [(r, min(POOL_CHUNK, n - r)) for r in range(0, n, POOL_CHUNK)]
    hbuf[0:h0] = jnp.where(pos > 0, norm(xp_ref[...]), zero)
    for r, n in chunks(tm):
        hbuf[h0 + r:h0 + r + n] = norm(x_ref[r:r + n])
    hbuf[h0 + tm:h0 + tm + POOL_HALO] = jnp.where(pos < tiles_per_seq - 1, norm(xn_ref[...]), zero)
    hbuf[h0 + tm + POOL_HALO:] = jnp.zeros((POOL_PAD, D_MODEL), F32)

    def pooled_rows(r0):
        for g, w in enumerate(POOL_WINDOWS):
            cols = slice(g * POOL_GROUP, (g + 1) * POOL_GROUP)
            n_levels = g
            src = lambda lo, hi: hbuf[r0 + lo:r0 + hi, cols]
            for k in range(n_levels):
                step = 2 ** k
                for r, n in chunks(ROW_BLOCK + POOL_HALO * (n_levels - k)):
                    sbuf[k, r:r + n] = src(r, r + n) + src(r + step, r + n + step)
                src = functools.partial(lambda k, lo, hi: sbuf[k, lo:hi], k)
            span = 2 ** n_levels
            lo = h0 - w // 2
            for r, n in chunks(ROW_BLOCK):
                total = functools.reduce(
                    jnp.add, [src(lo + r + j, lo + r + j + n) for j in range(0, w, span)])
                t = pos * tm + r0 + r + lax.broadcasted_iota(jnp.int32, (n, 1), 0)
                cnt = jnp.minimum(t - w // 2 + w, seq_len) - jnp.maximum(t - w // 2, 0)
                mean = total * (1.0 / cnt.astype(F32))
                rows = slice(r0 + r, r0 + r + n)
                abuf[rows, cols] = (mean - hbuf[h0 + r0 + r:h0 + r0 + r + n, cols]).astype(abuf.dtype)

    def mix(rows):
        a = abuf[rows, :]
        return jnp.concatenate(
            [_dot(a[:, g * POOL_GROUP:(g + 1) * POOL_GROUP], wp_ref[g])
             for g in range(len(POOL_WINDOWS))], axis=-1) * ps_ref[...]

    _residual_ffn(mix, x_ref, mod, n2g_ref, wg_ref, wu_ref, wd_ref, o_ref,
                  prepare=pooled_rows, group=POOL_PASS_BLOCKS)


def _pool_post(x, seq_len, tm, mod, rows_per_mod, n1g, n2g, w_pool, pool_scale, layer, wg, wu, wd):
    t = x.shape[0]
    hb = tm // POOL_HALO
    n_halo = t // POOL_HALO
    n_levels = len(POOL_WINDOWS) - 1
    return pl.pallas_call(
        functools.partial(_pool_post_kernel, seq_len),
        grid=(t // tm,),
        in_specs=[
            pl.BlockSpec((tm, D_MODEL), lambda i: (i, 0)),
            pl.BlockSpec((POOL_HALO, D_MODEL), lambda i: (jnp.maximum(i * hb - 1, 0), 0)),
            pl.BlockSpec((POOL_HALO, D_MODEL), lambda i: (jnp.minimum((i + 1) * hb, n_halo - 1), 0)),
            pl.BlockSpec((1, 6, D_MODEL), lambda i: (i * tm // rows_per_mod, 0, 0)),
            _const_spec((1, D_MODEL)),
            _const_spec((1, D_MODEL)),
            _const_spec(w_pool.shape),
            _const_spec((1, D_MODEL)),
        ] + [_layer_spec(w.shape, layer) for w in (wg, wu, wd)],
        out_specs=pl.BlockSpec((tm, D_MODEL), lambda i: (i, 0)),
        out_shape=jax.ShapeDtypeStruct((t, D_MODEL), F32),
        scratch_shapes=[
            pltpu.VMEM((tm + 2 * POOL_HALO + POOL_PAD, D_MODEL), F32),
            pltpu.VMEM((n_levels, ROW_BLOCK + POOL_HALO * n_levels, POOL_GROUP), F32),
            pltpu.VMEM((tm, D_MODEL), BF16),
        ],
        compiler_params=_params(),
        name="pool_post",
    )(x, x, x, mod, n1g, n2g, w_pool, pool_scale, wg, wu, wd)


def kernel(x, c, ctx, c_ctx, ada_w, ada_b, norm1_g, norm2_g, ffn_w_gate, ffn_w_up, ffn_w_down, na_w_qkv, na_w_o, na_q_norm, na_k_norm, na_rpb, gm_w_in, gm_b_in, gm_ln_g, gm_ln_b, gm_w_s, gm_b_s, gm_w_out, pool_w, pool_scale):
    b, seq, d = x.shape
    ctx_len = ctx.shape[1]
    assert d == D_MODEL and seq % POST_TM == 0 and (b * ctx_len) % POST_TM == 0 and seq % (Q_TILE * 2) == 0

    c_all = jnp.zeros((MOD_ROWS, d), F32).at[:b].set(c).at[b].set(c_ctx)
    mods = _ada_mods(c_all, ada_w, ada_b)

    heads_per_seg = 4
    seg = jnp.asarray(np.kron(np.eye(heads_per_seg), np.ones((NA_HEAD_DIM, NA_HEAD_DIM))), BF16)
    row = lambda v: v.reshape(1, d)
    tile_gain = lambda g: jnp.tile(g, NA_HEADS).reshape(1, d)

    wg, wu, wd = (w.astype(BF16) for w in (ffn_w_gate, ffn_w_up, ffn_w_down))
    xl = x.reshape(b * seq, d)
    xc = ctx.reshape(b * ctx_len, d)
    for i in range(DEPTH):
        kind, j = i % N_MIXERS, i // N_MIXERS
        last = i == DEPTH - 1
        need_ctx = (not last) or kind == 0
        mod_l = mods[i, :b].reshape(b, 6, d)
        mod_c = mods[i, b:b + 1].reshape(1, 6, d)
        streams = [(xl, mod_l, seq, seq)]
        if need_ctx:
            streams.append((xc, mod_c, b * ctx_len, ctx_len))
        n1g, n2g = row(norm1_g[i]), row(norm2_g[i])

        if kind == 2:
            outs = [_pool_post(xs, sl, min(sl, POST_TM), m, rpm, n1g, n2g, pool_w[j].astype(BF16),
                               row(pool_scale[j]), i, wg, wu, wd) for xs, m, rpm, sl in streams]
        elif kind == 0:
            w_qkv = na_w_qkv[j].astype(BF16)
            qg, kg = tile_gain(na_q_norm[j]), tile_gain(na_k_norm[j])
            (ql, kl, vl), (qc, kc, vc) = (
                tuple(t.reshape(b, -1, d) for t in _qkv(xs, m, rpm, n1g, w_qkv, qg, kg, seg))
                for xs, m, rpm, _ in streams)
            acts = _na_attention(ql, kl, vl, None if last else qc, kc, vc, _bias_ring(na_rpb[j]))
            acts = [a.reshape(-1, d) for a in (acts if isinstance(acts, (list, tuple)) else [acts])]
            mix_w = na_w_o[j].astype(BF16)
        else:
            w_in = gm_w_in[j].astype(BF16)
            b_in = gm_b_in[j].reshape(1, -1)
            w_s = gm_w_s[j].astype(BF16)
            b_s = jnp.broadcast_to(gm_b_s[j][:, :, None], (GMLP_GROUPS, GMLP_CHUNK, d // GMLP_GROUPS))
            acts = [_gmlp(xs, m, rpm, n1g, w_in, b_in, row(gm_ln_g[j]), row(gm_ln_b[j]), w_s, b_s)
                    for xs, m, rpm, _ in streams]
            mix_w = gm_w_out[j].astype(BF16)

        if kind != 2:
            outs = [_post(a, xs, m, rpm, n2g, mix_w, i, wg, wu, wd)
                    for a, (xs, m, rpm, _) in zip(acts, streams)]
        xl = outs[0]
        if not last:
            xc = outs[1]
    return xl.reshape(b, seq, d)
```

```python
import functools

import numpy as np
import jax
import jax.numpy as jnp
from jax import lax
from jax.experimental import pallas as pl
from jax.experimental.pallas import tpu as pltpu

D_MODEL = 1024
DEPTH = 4
GRID_W = 64
N_MIXERS = 3
EPS = 1e-6
NEG_INF = -1e30

NA_HEADS = 16
NA_HEAD_DIM = D_MODEL // NA_HEADS
NA_WIN_ROWS = 8
NA_WIN_COLS = 16
HEADS_PER_BLOCK = 2
PAIR_W = HEADS_PER_BLOCK * NA_HEAD_DIM
N_PAIRS = NA_HEADS // HEADS_PER_BLOCK
Q_TILE_ROWS = 4
Q_TILE = Q_TILE_ROWS * GRID_W
CTX_OUT_AT = 2
NA_BATCH_BLOCK = 2
LOG2_E = float(np.log2(np.e))
Q_SCALE = NA_HEAD_DIM ** -0.5 * LOG2_E
N_DR0 = 2 * NA_WIN_ROWS - 2
DR_LO = NA_WIN_ROWS - 1 - NA_WIN_ROWS // 2
DR_HI = DR_LO + NA_WIN_ROWS - 1

GMLP_CHUNK = 128
GMLP_GROUPS = 8
POOL_WINDOWS = (2, 4, 8, 16)
POOL_GROUP = D_MODEL // len(POOL_WINDOWS)
POOL_HALO = 8
POOL_PAD = 16
POOL_PASS_BLOCKS = 2
POOL_CHUNK = 32

MOD_ROWS = 32
ADA_TN = 1024
TM = 1024
ROW_BLOCK = 256
POST_TM = 1024
VMEM_LIMIT_BYTES = 56 * 1024 * 1024

BF16 = jnp.bfloat16
F32 = jnp.float32


def _dot(a, b):
    return jnp.dot(a, b, preferred_element_type=F32)


def _dot_nt(a, b):
    return lax.dot_general(a, b, (((1,), (1,)), ((), ())), preferred_element_type=F32)


def _silu(x):
    return x / (1.0 + jnp.exp(-x))


def _const_spec(shape):
    nd = len(shape)
    return pl.BlockSpec(shape, lambda *_: (0,) * nd, pipeline_mode=pl.Buffered(1))


def _layer_spec(shape, layer):
    nd = len(shape)
    return pl.BlockSpec((1,) + tuple(shape[1:]), lambda *_: (layer,) + (0,) * (nd - 1),
                        pipeline_mode=pl.Buffered(1))


def _params(n_axes=1):
    return pltpu.CompilerParams(
        dimension_semantics=("arbitrary",) * n_axes, vmem_limit_bytes=VMEM_LIMIT_BYTES)


def _norm_mod(x, g, sc, sh):
    ms = jnp.mean(x * x, axis=-1, keepdims=True)
    return (x * lax.rsqrt(ms + EPS)) * (g * (1.0 + sc)) + sh


def _ada_kernel(c_ref, w_ref, b_ref, o_ref):
    s = _silu(c_ref[...]).astype(BF16)
    o_ref[0] = _dot(s, w_ref[0].astype(BF16)) + b_ref[0]


def _ada_mods(c_all, ada_w, ada_b):
    n = ada_w.shape[-1]
    return pl.pallas_call(
        _ada_kernel,
        grid=(DEPTH, n // ADA_TN),
        in_specs=[
            pl.BlockSpec((MOD_ROWS, D_MODEL), lambda i, j: (0, 0)),
            pl.BlockSpec((1, D_MODEL, ADA_TN), lambda i, j: (i, 0, j)),
            pl.BlockSpec((1, 1, ADA_TN), lambda i, j: (i, 0, j)),
        ],
        out_specs=pl.BlockSpec((1, MOD_ROWS, ADA_TN), lambda i, j: (i, 0, j)),
        out_shape=jax.ShapeDtypeStruct((DEPTH, MOD_ROWS, n), F32),
        compiler_params=_params(2),
        name="ada_mods",
    )(c_all, ada_w, ada_b.reshape(DEPTH, 1, n))


def _residual_ffn(mix, x_ref, mod, n2g_ref, wg_ref, wu_ref, wd_ref, o_ref, prepare=None, group=None):
    g1, sh2, sc2, g2 = mod[2:3], mod[3:4], mod[4:5], mod[5:6]
    starts = list(range(0, x_ref.shape[0], ROW_BLOCK))
    group = group or len(starts)
    passes = [starts[p:p + group] for p in range(0, len(starts), group)]
    if prepare is not None:
        for r0 in passes[0]:
            prepare(r0)
    for p, pass_starts in enumerate(passes):
        blocks = [pl.ds(r0, ROW_BLOCK) for r0 in pass_starts]
        ys = [mix(rows) for rows in blocks]
        x1s = [x_ref[rows, :] + g1 * y for rows, y in zip(blocks, ys)]
        h2s = [_norm_mod(x1, n2g_ref[...], sc2, sh2).astype(BF16) for x1 in x1s]
        gus = [(_dot(h2, wg_ref[0]), _dot(h2, wu_ref[0])) for h2 in h2s]
        if prepare is not None and p + 1 < len(passes):
            for r0 in passes[p + 1]:
                prepare(r0)
        acts = [(_silu(gate) * up).astype(BF16) for gate, up in gus]
        for rows, x1, act in zip(blocks, x1s, acts):
            o_ref[rows, :] = x1 + g2 * _dot(act, wd_ref[0])


def _post_kernel(a_ref, x_ref, mod_ref, n2g_ref, wm_ref, wg_ref, wu_ref, wd_ref, o_ref):
    mix = lambda rows: _dot(a_ref[rows, :], wm_ref[...])
    _residual_ffn(mix, x_ref, mod_ref[0], n2g_ref, wg_ref, wu_ref, wd_ref, o_ref)


def _post(a, x, mod, rows_per_mod, n2g, mix_w, layer, wg, wu, wd):
    t = x.shape[0]
    tok = lambda i: (i, 0)
    return pl.pallas_call(
        _post_kernel,
        grid=(t // POST_TM,),
        in_specs=[
            pl.BlockSpec((POST_TM, D_MODEL), tok),
            pl.BlockSpec((POST_TM, D_MODEL), tok),
            pl.BlockSpec((1, 6, D_MODEL), lambda i: (i * POST_TM // rows_per_mod, 0, 0)),
            _const_spec((1, D_MODEL)),
            _const_spec(mix_w.shape),
        ] + [_layer_spec(w.shape, layer) for w in (wg, wu, wd)],
        out_specs=pl.BlockSpec((POST_TM, D_MODEL), tok),
        out_shape=jax.ShapeDtypeStruct((t, D_MODEL), F32),
        compiler_params=_params(),
        name="post_dense",
    )(a, x, mod, n2g, mix_w, wg, wu, wd)


def _qkv_kernel(x_ref, mod_ref, n1g_ref, w_ref, qg_ref, kg_ref, seg_ref, q_ref, k_ref, v_ref):
    mod = mod_ref[0]
    seg = seg_ref[...]
    seg_w = seg.shape[0]

    def mean_sq(t):
        return jnp.concatenate(
            [_dot((t[:, c:c + seg_w] * t[:, c:c + seg_w]).astype(BF16), seg)
             for c in range(0, D_MODEL, seg_w)], axis=-1) * (1.0 / NA_HEAD_DIM)

    blocks = [pl.ds(r, ROW_BLOCK) for r in range(0, x_ref.shape[0], ROW_BLOCK)]
    hs = [_norm_mod(x_ref[rows, :], n1g_ref[...], mod[1:2], mod[0:1]).astype(BF16) for rows in blocks]
    qkvs = [_dot(h, w_ref[...]) for h in hs]
    for rows, qkv in zip(blocks, qkvs):
        v_ref[rows, :] = qkv[:, 2 * D_MODEL:].astype(BF16)
    qs = [qkv[:, :D_MODEL] for qkv in qkvs]
    ks = [qkv[:, D_MODEL:2 * D_MODEL] for qkv in qkvs]
    q_ms = [mean_sq(q) for q in qs]
    k_ms = [mean_sq(k) for k in ks]
    for rows, q, k, qm, km in zip(blocks, qs, ks, q_ms, k_ms):
        q_ref[rows, :] = (q * lax.rsqrt(qm + EPS) * qg_ref[...] * Q_SCALE).astype(BF16)
        k_ref[rows, :] = (k * lax.rsqrt(km + EPS) * kg_ref[...]).astype(BF16)


def _qkv(x, mod, rows_per_mod, n1g, w_qkv, qg, kg, seg):
    t = x.shape[0]
    tok = lambda i: (i, 0)
    out = jax.ShapeDtypeStruct((t, D_MODEL), BF16)
    return pl.pallas_call(
        _qkv_kernel,
        grid=(t // TM,),
        in_specs=[
            pl.BlockSpec((TM, D_MODEL), tok),
            pl.BlockSpec((1, 6, D_MODEL), lambda i: (i * TM // rows_per_mod, 0, 0)),
            _const_spec((1, D_MODEL)),
            _const_spec(w_qkv.shape),
            _const_spec((1, D_MODEL)),
            _const_spec((1, D_MODEL)),
            _const_spec(seg.shape),
        ],
        out_specs=[pl.BlockSpec((TM, D_MODEL), tok)] * 3,
        out_shape=[out, out, out],
        compiler_params=_params(),
        name="qkv",
    )(x, mod, n1g, w_qkv, qg, kg, seg)


def _softmax_rows(s_parts, biases):
    biased = [s if b is None else s + b for s, b in zip(s_parts, biases)]
    m = jnp.max(functools.reduce(jnp.maximum, biased), axis=-1, keepdims=True)
    p_parts = [jnp.exp2(s - m) for s in biased]
    l = jnp.sum(functools.reduce(jnp.add, p_parts), axis=-1, keepdims=True)
    return p_parts, l


def _stack_heads(qt, first):
    zero = jnp.zeros_like(qt)
    return jnp.concatenate([jnp.where(first, qt, zero), jnp.where(first, zero, qt)], axis=0)


def _na_kernel(with_ctx_out, q_ref, k_ref, v_ref, *rest):
    if with_ctx_out:
        qc_ref, kc_ref, vc_ref, ring_ref, mask_ref, o_ref, oc_ref, tab_ref = rest
    else:
        kc_ref, vc_ref, ring_ref, mask_ref, o_ref, tab_ref = rest
    first = lax.broadcasted_iota(jnp.int32, (1, PAIR_W), 1) < NA_HEAD_DIM

    @pl.when(pl.program_id(1) == 0)
    def _():
        for hh in range(HEADS_PER_BLOCK):
            def toeplitz(dr, shift):
                rows = jnp.broadcast_to(ring_ref[hh, dr:dr + 1, :], (GRID_W, PAIR_W))
                return pltpu.roll(rows, shift, 1, stride=1, stride_axis=0)
            pair = lambda dr0: jnp.where(first, toeplitz(dr0, 0), toeplitz(dr0 + 1, NA_HEAD_DIM))
            for dr0 in range(N_DR0):
                tab_ref[hh, dr0] = pair(dr0) * LOG2_E + mask_ref[0]
            tab_ref[hh, N_DR0] = pair(DR_LO - 1) * LOG2_E + mask_ref[1]
            tab_ref[hh, N_DR0 + 1] = pair(DR_HI) * LOG2_E + mask_ref[2]

    for bb in range(q_ref.shape[0]):
        _na_batch(first, tab_ref, q_ref.at[bb], k_ref.at[bb], v_ref.at[bb],
                  qc_ref.at[bb] if with_ctx_out else None, kc_ref.at[bb], vc_ref.at[bb],
                  o_ref.at[bb], oc_ref.at[bb] if with_ctx_out else None)


def _na_batch(first, tab_ref, q_ref, k_ref, v_ref, qc_ref, kc_ref, vc_ref, o_ref, oc_ref):
    seq = q_ref.shape[0]
    n_tiles = seq // Q_TILE
    kc = kc_ref[...]
    vc = vc_ref[...]
    ctx_parts = range(0, kc.shape[0], PAIR_W)

    def tab_index(dr0, clamped):
        if not clamped and dr0 == DR_LO - 1:
            return N_DR0
        if not clamped and dr0 == DR_HI:
            return N_DR0 + 1
        return dr0

    def context_output():
        n = qc_ref.shape[0]
        (p,), l = _softmax_rows([_dot_nt(_stack_heads(qc_ref[...], first), kc)], [None])
        oc = _dot(p.astype(BF16), vc) * (1.0 / l)
        oc_ref[...] = jnp.where(first, oc[:n], oc[n:]).astype(oc_ref.dtype)

    def scores(q0, k0, win_rows, *_):
        qs = _stack_heads(q_ref[pl.ds(q0, Q_TILE), :], first)
        return _dot_nt(qs, k_ref[pl.ds(k0, win_rows * GRID_W), :]), _dot_nt(qs, kc)

    def probs(s_loc, s_ctx, q0, k0, win_rows, dr_base, clamped):
        p_loc, p_ctx, inv_l = [], [], []
        zero = jnp.zeros((GRID_W, PAIR_W), BF16)
        for hh in range(HEADS_PER_BLOCK):
            for i in range(Q_TILE_ROWS):
                r0 = hh * Q_TILE + i * GRID_W
                lo, hi = (0, win_rows) if clamped else (i, i + NA_WIN_ROWS)
                pairs = range(lo // 2, (hi + 1) // 2)
                parts = [s_loc[r0:r0 + GRID_W, wp * PAIR_W:(wp + 1) * PAIR_W] for wp in pairs]
                biases = [tab_ref[hh, tab_index(2 * wp - i + dr_base, clamped)] for wp in pairs]
                parts += [s_ctx[r0:r0 + GRID_W, c:c + PAIR_W] for c in ctx_parts]
                biases += [None] * len(ctx_parts)
                p_parts, l = _softmax_rows(parts, biases)
                p_loc.append(jnp.concatenate(
                    [p_parts[pairs.index(wp)].astype(BF16) if wp in pairs else zero
                     for wp in range(win_rows // 2)], axis=-1))
                p_ctx.append(jnp.concatenate([p.astype(BF16) for p in p_parts[len(pairs):]], axis=-1))
                inv_l.append(1.0 / l)
        return (jnp.concatenate(p_loc, axis=0), jnp.concatenate(p_ctx, axis=0),
                jnp.concatenate(inv_l, axis=0))

    def output(p_loc, p_ctx, inv_l, q0, k0, win_rows, *_):
        o = (_dot(p_loc, v_ref[pl.ds(k0, win_rows * GRID_W), :]) + _dot(p_ctx, vc)) * inv_l
        o_ref[pl.ds(q0, Q_TILE), :] = jnp.where(first, o[:Q_TILE], o[Q_TILE:]).astype(o_ref.dtype)

    tiles = [(0, 0, NA_WIN_ROWS, NA_WIN_ROWS - 1, True),
             (seq - Q_TILE, seq - NA_WIN_ROWS * GRID_W, NA_WIN_ROWS, Q_TILE_ROWS - 1, True)]
    tiles += [(t * Q_TILE, t * Q_TILE - (NA_WIN_ROWS // 2) * GRID_W, NA_WIN_ROWS + Q_TILE_ROWS, DR_LO, False)
              for t in range(1, n_tiles - 1)]
    for n, cfg in enumerate(tiles):
        if qc_ref is not None and n == CTX_OUT_AT:
            context_output()
        output(*probs(*scores(*cfg), *cfg), *cfg)


def _na_attention(q, k, v, qc, kc, vc, ring):
    b, seq, _ = q.shape
    ctx_len = kc.shape[1]
    lat = pl.BlockSpec((NA_BATCH_BLOCK, seq, PAIR_W), lambda j, i: (i, 0, j))
    ctx = pl.BlockSpec((NA_BATCH_BLOCK, ctx_len, PAIR_W), lambda j, i: (i, 0, j))
    ring_spec = pl.BlockSpec((HEADS_PER_BLOCK,) + ring.shape[1:], lambda j, i: (j, 0, 0))
    masks = _bias_masks()
    with_ctx_out = qc is not None
    lat_shape = jax.ShapeDtypeStruct((b, seq, D_MODEL), BF16)
    ctx_shape = jax.ShapeDtypeStruct((b, ctx_len, D_MODEL), BF16)
    return pl.pallas_call(
        functools.partial(_na_kernel, with_ctx_out),
        grid=(N_PAIRS, b // NA_BATCH_BLOCK),
        in_specs=[lat, lat, lat] + [ctx] * (3 if with_ctx_out else 2) + [ring_spec, _const_spec(masks.shape)],
        out_specs=[lat, ctx] if with_ctx_out else lat,
        out_shape=[lat_shape, ctx_shape] if with_ctx_out else lat_shape,
        scratch_shapes=[pltpu.VMEM((HEADS_PER_BLOCK, N_DR0 + 2, GRID_W, PAIR_W), F32)],
        compiler_params=_params(2),
        name="na_attention",
    )(q, k, v, *([qc] if with_ctx_out else []), kc, vc, ring, masks)


def _bias_ring(rpb):
    n_dc = 2 * NA_WIN_COLS - 1
    return jnp.concatenate([rpb[..., NA_WIN_COLS - 1:],
                            jnp.zeros(rpb.shape[:2] + (PAIR_W - n_dc,), rpb.dtype),
                            rpb[..., :NA_WIN_COLS - 1]], axis=-1).astype(F32)


def _bias_masks():
    qcol = np.arange(GRID_W)[:, None]
    kcol = np.arange(GRID_W)[None, :]
    cstart = np.clip(qcol - NA_WIN_COLS // 2, 0, GRID_W - NA_WIN_COLS)
    col_ok = np.tile((kcol >= cstart) & (kcol < cstart + NA_WIN_COLS), (1, 2))
    second = np.arange(PAIR_W)[None, :] >= GRID_W
    oks = [col_ok, col_ok & second, col_ok & ~second]
    return jnp.asarray(np.stack([np.where(ok, 0.0, NEG_INF) for ok in oks]), F32)


def _gelu_tanh(x):
    c = float(np.sqrt(2.0 / np.pi))
    return x * (0.5 + 0.5 * jnp.tanh(x * (c + (c * 0.044715) * (x * x))))


def _gmlp_kernel(x_ref, mod_ref, n1g_ref, win_ref, bin_ref, lng_ref, lnb_ref, ws_ref, bs_ref, a_ref):
    mod = mod_ref[0]
    gw = D_MODEL // GMLP_GROUPS

    def layer_norm(v):
        mu = jnp.mean(v, axis=-1, keepdims=True)
        vc = v - mu
        var = jnp.mean(vc * vc, axis=-1, keepdims=True)
        return (vc * lax.rsqrt(var + EPS) * lng_ref[...] + lnb_ref[...]).astype(BF16)

    rows_per_block = ROW_BLOCK
    blocks = list(range(0, x_ref.shape[0], ROW_BLOCK))
    hs = [_norm_mod(x_ref[pl.ds(r0, rows_per_block), :], n1g_ref[...], mod[1:2], mod[0:1]).astype(BF16)
          for r0 in blocks]
    zs = [_dot(h, win_ref[...]) for h in hs]
    zs = [_gelu_tanh(z + bin_ref[...]) for z in zs]
    vns = [layer_norm(z[:, D_MODEL:]) for z in zs]
    chunks = [slice(n * GMLP_CHUNK, (n + 1) * GMLP_CHUNK) for n in range(rows_per_block // GMLP_CHUNK)]
    for r0, z, vn in zip(blocks, zs, vns):
        for g in range(GMLP_GROUPS):
            cols = slice(g * gw, (g + 1) * gw)
            mixed = _dot(ws_ref[g], jnp.concatenate([vn[rows, cols] for rows in chunks], axis=1))
            for n, rows in enumerate(chunks):
                a_ref[pl.ds(r0 + rows.start, GMLP_CHUNK), cols] = (
                    z[rows, cols] * (mixed[:, n * gw:(n + 1) * gw] + bs_ref[g])).astype(a_ref.dtype)


def _gmlp(x, mod, rows_per_mod, n1g, w_in, b_in, ln_g, ln_b, w_s, b_s):
    t = x.shape[0]
    tok = lambda i: (i, 0)
    return pl.pallas_call(
        _gmlp_kernel,
        grid=(t // TM,),
        in_specs=[
            pl.BlockSpec((TM, D_MODEL), tok),
            pl.BlockSpec((1, 6, D_MODEL), lambda i: (i * TM // rows_per_mod, 0, 0)),
            _const_spec((1, D_MODEL)),
            _const_spec(w_in.shape),
            _const_spec(b_in.shape),
            _const_spec((1, D_MODEL)),
            _const_spec((1, D_MODEL)),
            _const_spec(w_s.shape),
            _const_spec(b_s.shape),
        ],
        out_specs=pl.BlockSpec((TM, D_MODEL), tok),
        out_shape=jax.ShapeDtypeStruct((t, D_MODEL), BF16),
        compiler_params=_params(),
        name="gmlp",
    )(x, mod, n1g, w_in, b_in, ln_g, ln_b, w_s, b_s)


def _pool_post_kernel(seq_len, x_ref, xp_ref, xn_ref, mod_ref, n1g_ref, n2g_ref, wp_ref, ps_ref,
                      wg_ref, wu_ref, wd_ref, o_ref, hbuf, sbuf, abuf):
    tm = x_ref.shape[0]
    tiles_per_seq = seq_len // tm
    pos = pl.program_id(0) % tiles_per_seq
    mod = mod_ref[0]
    n1g = n1g_ref[...]
    norm = lambda x: _norm_mod(x, n1g, mod[1:2], mod[0:1])
    zero = jnp.zeros((POOL_HALO, D_MODEL), F32)
    h0 = POOL_HALO
    chunks = lambda n: [(r, min(POOL_CHUNK, n - r)) for r in range(0, n, POOL_CHUNK)]
    hbuf[0:h0] = jnp.where(pos > 0, norm(xp_ref[...]), zero)
    for r, n in chunks(tm):
        hbuf[h0 + r:h0 + r + n] = norm(x_ref[r:r + n])
    hbuf[h0 + tm:h0 + tm + POOL_HALO] = jnp.where(pos < tiles_per_seq - 1, norm(xn_ref[...]), zero)
    hbuf[h0 + tm + POOL_HALO:] = jnp.zeros((POOL_PAD, D_MODEL), F32)

    def pooled_rows(r0):
        for g, w in enumerate(POOL_WINDOWS):
            cols = slice(g * POOL_GROUP, (g + 1) * POOL_GROUP)
            n_levels = g
            src = lambda lo, hi: hbuf[r0 + lo:r0 + hi, cols]
            for k in range(n_levels):
                step = 2 ** k
                for r, n in chunks(ROW_BLOCK + POOL_HALO * (n_levels - k)):
                    sbuf[k, r:r + n] = src(r, r + n) + src(r + step, r + n + step)
                src = functools.partial(lambda k, lo, hi: sbuf[k, lo:hi], k)
            span = 2 ** n_levels
            lo = h0 - w // 2
            for r, n in chunks(ROW_BLOCK):
                total = functools.reduce(
                    jnp.add, [src(lo + r + j, lo + r + j + n) for j in range(0, w, span)])
                t = pos * tm + r0 + r + lax.broadcasted_iota(jnp.int32, (n, 1), 0)
                cnt = jnp.minimum(t - w // 2 + w, seq_len) - jnp.maximum(t - w // 2, 0)
                mean = total * (1.0 / cnt.astype(F32))
                rows = slice(r0 + r, r0 + r + n)
                abuf[rows, cols] = (mean - hbuf[h0 + r0 + r:h0 + r0 + r + n, cols]).astype(abuf.dtype)

    def mix(rows):
        a = abuf[rows, :]
        return jnp.concatenate(
            [_dot(a[:, g * POOL_GROUP:(g + 1) * POOL_GROUP], wp_ref[g])
             for g in range(len(POOL_WINDOWS))], axis=-1) * ps_ref[...]

    _residual_ffn(mix, x_ref, mod, n2g_ref, wg_ref, wu_ref, wd_ref, o_ref,
                  prepare=pooled_rows, group=POOL_PASS_BLOCKS)


def _pool_post(x, seq_len, tm, mod, rows_per_mod, n1g, n2g, w_pool, pool_scale, layer, wg, wu, wd):
    t = x.shape[0]
    hb = tm // POOL_HALO
    n_halo = t // POOL_HALO
    n_levels = len(POOL_WINDOWS) - 1
    return pl.pallas_call(
        functools.partial(_pool_post_kernel, seq_len),
        grid=(t // tm,),
        in_specs=[
            pl.BlockSpec((tm, D_MODEL), lambda i: (i, 0)),
            pl.BlockSpec((POOL_HALO, D_MODEL), lambda i: (jnp.maximum(i * hb - 1, 0), 0)),
            pl.BlockSpec((POOL_HALO, D_MODEL), lambda i: (jnp.minimum((i + 1) * hb, n_halo - 1), 0)),
            pl.BlockSpec((1, 6, D_MODEL), lambda i: (i * tm // rows_per_mod, 0, 0)),
            _const_spec((1, D_MODEL)),
            _const_spec((1, D_MODEL)),
            _const_spec(w_pool.shape),
            _const_spec((1, D_MODEL)),
        ] + [_layer_spec(w.shape, layer) for w in (wg, wu, wd)],
        out_specs=pl.BlockSpec((tm, D_MODEL), lambda i: (i, 0)),
        out_shape=jax.ShapeDtypeStruct((t, D_MODEL), F32),
        scratch_shapes=[
            pltpu.VMEM((tm + 2 * POOL_HALO + POOL_PAD, D_MODEL), F32),
            pltpu.VMEM((n_levels, ROW_BLOCK + POOL_HALO * n_levels, POOL_GROUP), F32),
            pltpu.VMEM((tm, D_MODEL), BF16),
        ],
        compiler_params=_params(),
        name="pool_post",
    )(x, x, x, mod, n1g, n2g, w_pool, pool_scale, wg, wu, wd)


def kernel(x, c, ctx, c_ctx, ada_w, ada_b, norm1_g, norm2_g, ffn_w_gate, ffn_w_up, ffn_w_down, na_w_qkv, na_w_o, na_q_norm, na_k_norm, na_rpb, gm_w_in, gm_b_in, gm_ln_g, gm_ln_b, gm_w_s, gm_b_s, gm_w_out, pool_w, pool_scale):
    b, seq, d = x.shape
    ctx_len = ctx.shape[1]
    assert d == D_MODEL and seq % POST_TM == 0 and (b * ctx_len) % POST_TM == 0 and seq % (Q_TILE * 2) == 0

    c_all = jnp.zeros((MOD_ROWS, d), F32).at[:b].set(c).at[b].set(c_ctx)
    mods = _ada_mods(c_all, ada_w, ada_b)

    heads_per_seg = 4
    seg = jnp.asarray(np.kron(np.eye(heads_per_seg), np.ones((NA_HEAD_DIM, NA_HEAD_DIM))), BF16)
    row = lambda v: v.reshape(1, d)
    tile_gain = lambda g: jnp.tile(g, NA_HEADS).reshape(1, d)

    wg, wu, wd = (w.astype(BF16) for w in (ffn_w_gate, ffn_w_up, ffn_w_down))
    xl = x.reshape(b * seq, d)
    xc = ctx.reshape(b * ctx_len, d)
    for i in range(DEPTH):
        kind, j = i % N_MIXERS, i // N_MIXERS
        last = i == DEPTH - 1
        need_ctx = (not last) or kind == 0
        mod_l = mods[i, :b].reshape(b, 6, d)
        mod_c = mods[i, b:b + 1].reshape(1, 6, d)
        streams = [(xl, mod_l, seq, seq)]
        if need_ctx:
            streams.append((xc, mod_c, b * ctx_len, ctx_len))
        n1g, n2g = row(norm1_g[i]), row(norm2_g[i])

        if kind == 2:
            outs = [_pool_post(xs, sl, min(sl, POST_TM), m, rpm, n1g, n2g, pool_w[j].astype(BF16),
                               row(pool_scale[j]), i, wg, wu, wd) for xs, m, rpm, sl in streams]
        elif kind == 0:
            w_qkv = na_w_qkv[j].astype(BF16)
            qg, kg = tile_gain(na_q_norm[j]), tile_gain(na_k_norm[j])
            (ql, kl, vl), (qc, kc, vc) = (
                tuple(t.reshape(b, -1, d) for t in _qkv(xs, m, rpm, n1g, w_qkv, qg, kg, seg))
                for xs, m, rpm, _ in streams)
            acts = _na_attention(ql, kl, vl, None if last else qc, kc, vc, _bias_ring(na_rpb[j]))
            acts = [a.reshape(-1, d) for a in (acts if isinstance(acts, (list, tuple)) else [acts])]
            mix_w = na_w_o[j].astype(BF16)
        else:
            w_in = gm_w_in[j].astype(BF16)
            b_in = gm_b_in[j].reshape(1, -1)
            w_s = gm_w_s[j].astype(BF16)
            b_s = jnp.broadcast_to(gm_b_s[j][:, :, None], (GMLP_GROUPS, GMLP_CHUNK, d // GMLP_GROUPS))
            acts = [_gmlp(xs, m, rpm, n1g, w_in, b_in, row(gm_ln_g[j]), row(gm_ln_b[j]), w_s, b_s)
                    for xs, m, rpm, _ in streams]
            mix_w = gm_w_out[j].astype(BF16)

        if kind != 2:
            outs = [_post(a, xs, m, rpm, n2g, mix_w, i, wg, wu, wd)
                    for a, (xs, m, rpm, _) in zip(acts, streams)]
        xl = outs[0]
        if not last:
            xc = outs[1]
    return xl.reshape(b, seq, d)
```
